```python
import jax, jax.numpy as jnp
from jax import lax
import numpy as np

D_MODEL = 1024
BATCH = 8
SEQ = 2048
DEPTH = 4
DEC_BATCH = 128
DEC_SEQ = 4
PAST_LEN = 8192
PAGE_SIZE = 128

CONV_DIM = 512
CONV_WIDTH = 31
GLA_HEADS = 4
GLA_DK = 64
GLA_DV = 128
GLA_GATE_RANK = 16
GLA_TAU = 16.0
GLA_CHUNK = 64
GMLP_DIM = 512
GMLP_GROUPS = 4
GMLP_CHUNK = 128
GMLP_GC = GMLP_DIM // GMLP_GROUPS
MLA_HEADS = 8
MLA_Q_RANK = 384
MLA_KV_RANK = 256
MLA_NOPE = 64
MLA_ROPE = 32
MLA_V = 64
MLA_SCALE = (MLA_NOPE + MLA_ROPE) ** -0.5
ROPE_THETA = 10000.0
ATTN_BLOCK = 128
D_FF = 4 * D_MODEL
N_BRANCH = 4
EPS = 1e-6

IN_CONV = 2 * CONV_DIM
IN_GLA = 2 * GLA_HEADS * GLA_DK + 2 * GLA_HEADS * GLA_DV + GLA_GATE_RANK
IN_GMLP = 2 * GMLP_DIM
IN_MLA = MLA_Q_RANK + MLA_KV_RANK + MLA_ROPE
IN_GATE = N_BRANCH * D_MODEL
IN_TOTAL = IN_CONV + IN_GLA + IN_GMLP + IN_MLA + IN_GATE
IN_SPLITS = [IN_CONV, IN_CONV + IN_GLA, IN_CONV + IN_GLA + IN_GMLP, IN_CONV + IN_GLA + IN_GMLP + IN_MLA]
GLA_SPLITS = [GLA_HEADS * GLA_DK, 2 * GLA_HEADS * GLA_DK, 2 * GLA_HEADS * GLA_DK + GLA_HEADS * GLA_DV,
              2 * GLA_HEADS * GLA_DK + GLA_HEADS * GLA_DV + GLA_GATE_RANK]

kernel_name = 'hybrid_conv_gla_gmlp_mla_step'

F32 = jnp.float32


def rmsnorm(x, g):
    xf = x.astype(F32)
    return (xf * lax.rsqrt(jnp.mean(xf * xf, -1, keepdims=True) + EPS) * g).astype(x.dtype)


def layernorm(x, g, b):
    xf = x.astype(F32)
    mu = jnp.mean(xf, -1, keepdims=True)
    var = jnp.mean(jnp.square(xf - mu), -1, keepdims=True)
    return ((xf - mu) * lax.rsqrt(var + EPS) * g + b).astype(x.dtype)


def rope(x, pos):
    half = x.shape[-1] // 2
    inv = ROPE_THETA ** (-jnp.arange(half, dtype=F32) / half)
    ang = pos.astype(F32)[:, None] * inv[None, :]
    cos = jnp.cos(ang)[None, :, None, :]
    sin = jnp.sin(ang)[None, :, None, :]
    xf = x.astype(F32)
    x1, x2 = xf[..., :half], xf[..., half:]
    return jnp.concatenate([x1 * cos - x2 * sin, x2 * cos + x1 * sin], -1).astype(x.dtype)


def conv_branch(z, buf, w_dw, b_dw, ln_g, ln_b, w_out):
    a, gt = jnp.split(z, 2, axis=-1)
    glu = a * jax.nn.sigmoid(gt)
    full = jnp.concatenate([buf.astype(glu.dtype), glu], axis=1)
    y = lax.conv_general_dilated(full, w_dw[:, None, :].astype(full.dtype), window_strides=(1,), padding='VALID',
                                 dimension_numbers=('NWC', 'WIO', 'NWC'), feature_group_count=CONV_DIM) + b_dw
    y = jax.nn.silu(layernorm(y, ln_g, ln_b))
    return y @ w_out, full[:, -(CONV_WIDTH - 1):]


def gla_chunked(q, k, v, g, S0):
    B, L, H, _ = q.shape
    C = min(GLA_CHUNK, L)
    n = -(-L // C)
    pad = n * C - L

    def chunks(t):
        t = jnp.pad(t.astype(F32), ((0, 0), (0, pad), (0, 0), (0, 0)))
        return t.reshape(B, n, C, H, t.shape[-1]).transpose(1, 0, 3, 2, 4)

    qc, kc, vc, gc = chunks(q), chunks(k), chunks(v), chunks(g)
    mask = jnp.tril(jnp.ones((C, C), dtype=bool))[:, :, None]

    def step(S, inp):
        qi, ki, vi, gi = inp
        b = jnp.cumsum(gi, axis=2)
        diff = b[:, :, :, None, :] - b[:, :, None, :, :]
        decay = jnp.exp(jnp.where(mask, diff, -jnp.inf))
        A = jnp.einsum('bhik,bhjk,bhijk->bhij', qi, ki, decay)
        o = jnp.einsum('bhij,bhjv->bhiv', A, vi) + jnp.einsum('bhik,bhkv->bhiv', qi * jnp.exp(b), S)
        b_last = b[:, :, -1:, :]
        S = jnp.exp(b_last[:, :, 0, :, None]) * S + jnp.einsum('bhjk,bhjv->bhkv', ki * jnp.exp(b_last - b), vi)
        return S, o

    S, o = lax.scan(step, S0.astype(F32), (qc, kc, vc, gc))
    o = o.transpose(1, 0, 3, 2, 4).reshape(B, n * C, H, v.shape[-1])[:, :L]
    return o.astype(v.dtype), S.astype(S0.dtype)


def gla_branch(z, S0, w_gate_up, b_gate_up, norm_g, w_out):
    B, L, _ = z.shape
    q, k, v, gd, r = jnp.split(z, GLA_SPLITS, axis=-1)
    q = q.reshape(B, L, GLA_HEADS, GLA_DK) * (GLA_DK ** -0.5)
    k = k.reshape(B, L, GLA_HEADS, GLA_DK)
    v = v.reshape(B, L, GLA_HEADS, GLA_DV)
    g = jax.nn.log_sigmoid((gd @ w_gate_up + b_gate_up).astype(F32)) / GLA_TAU
    g = g.reshape(B, L, GLA_HEADS, GLA_DK)
    o, S = gla_chunked(q, k, v, g, S0)
    o = rmsnorm(o, norm_g).reshape(B, L, GLA_HEADS * GLA_DV) * jax.nn.silu(r)
    return o @ w_out, S


def gmlp_branch(z, ln_g, ln_b, w_s, b_s, w_out):
    u, v = jnp.split(jax.nn.gelu(z), 2, axis=-1)
    v = layernorm(v, ln_g, ln_b)
    B, L, _ = v.shape
    n = -(-L // GMLP_CHUNK)
    pad = n * GMLP_CHUNK - L
    vc = jnp.pad(v, ((0, 0), (0, pad), (0, 0))).reshape(B, n, GMLP_CHUNK, GMLP_GROUPS, GMLP_GC)
    s = jnp.einsum('gts,bnsgc->bntgc', jnp.tril(w_s), vc) + b_s.T[None, None, :, :, None]
    s = s.reshape(B, n * GMLP_CHUNK, GMLP_DIM)[:, :L]
    return (u * s) @ w_out, v


def _scores(q, k):
    if k.ndim == 3:
        return jnp.einsum('bqhe,bke->bhqk', q, k, preferred_element_type=F32)
    return jnp.einsum('bqhe,bkhe->bhqk', q, k, preferred_element_type=F32)


def attend(q_parts, k_parts, v, q_pos, k_pos):
    B, Lq, H, _ = q_parts[0].shape
    qb = min(ATTN_BLOCK, Lq)
    nb = -(-Lq // qb)
    pad = nb * qb - Lq

    def blocks(t):
        t = jnp.pad(t, ((0, 0), (0, pad), (0, 0), (0, 0)))
        return t.reshape(B, nb, qb, H, t.shape[-1]).swapaxes(0, 1)

    qs = tuple(blocks(q) for q in q_parts)
    qp = jnp.pad(q_pos, (0, pad), mode='edge').reshape(nb, qb)

    def one(args):
        qbs, pb = args
        s = _scores(qbs[0], k_parts[0])
        for qq, kk in zip(qbs[1:], k_parts[1:]):
            s = s + _scores(qq, kk)
        s = jnp.where(k_pos[None, None, None, :] <= pb[None, None, :, None], s * MLA_SCALE, -jnp.inf)
        p = jax.nn.softmax(s, axis=-1).astype(v.dtype)
        if v.ndim == 3:
            return jnp.einsum('bhqk,bkf->bqhf', p, v)
        return jnp.einsum('bhqk,bkhf->bqhf', p, v)

    o = lax.map(one, (qs, qp))
    return o.swapaxes(0, 1).reshape(B, nb * qb, H, o.shape[-1])[:, :Lq]


def mla_branch(z, pos, past_lat, past_rope, past_pos, q_norm_g, w_uq, kv_norm_g, w_uk, w_uv, w_out):
    B, L, _ = z.shape
    cq, ckv, kr = jnp.split(z, [MLA_Q_RANK, MLA_Q_RANK + MLA_KV_RANK], axis=-1)
    q = (rmsnorm(cq, q_norm_g) @ w_uq).reshape(B, L, MLA_HEADS, MLA_NOPE + MLA_ROPE)
    q_nope, q_rope = q[..., :MLA_NOPE], rope(q[..., MLA_NOPE:], pos)
    c = rmsnorm(ckv, kv_norm_g)
    k_rope = rope(kr[:, :, None, :], pos)[:, :, 0, :]
    if past_lat is None:
        k_nope = jnp.einsum('blr,rhd->blhd', c, w_uk)
        v = jnp.einsum('blr,rhd->blhd', c, w_uv)
        o = attend((q_nope, q_rope), (k_nope, k_rope), v, pos, pos)
    else:
        c_all = jnp.concatenate([past_lat.astype(c.dtype), c], axis=1)
        r_all = jnp.concatenate([past_rope.astype(k_rope.dtype), k_rope], axis=1)
        k_pos = jnp.concatenate([past_pos, pos])
        q_lat = jnp.einsum('blhd,rhd->blhr', q_nope, w_uk)
        o_lat = attend((q_lat, q_rope), (c_all, r_all), c_all, pos, k_pos)
        o = jnp.einsum('blhr,rhd->blhd', o_lat, w_uv)
    return o.reshape(B, L, MLA_HEADS * MLA_V) @ w_out, c, k_rope


def trunk_layer(x, pos, conv_buf, gla_S, past_lat, past_rope, past_pos, p):
    B, L, _ = x.shape
    h = rmsnorm(x, p['g_pre_mix'])
    z = h @ p['w_in']
    z_conv, z_gla, z_gmlp, z_mla, z_gate = jnp.split(z, IN_SPLITS, axis=-1)
    br_a, conv_buf = conv_branch(z_conv, conv_buf, p['conv_w'], p['conv_b'], p['conv_ln_g'], p['conv_ln_b'], p['conv_out'])
    br_b, gla_S = gla_branch(z_gla, gla_S, p['gla_gate_up'], p['gla_gate_b'], p['gla_norm_g'], p['gla_out'])
    br_c, v_rows = gmlp_branch(z_gmlp, p['gmlp_ln_g'], p['gmlp_ln_b'], p['gmlp_ws'], p['gmlp_bs'], p['gmlp_out'])
    br_d, lat, krope = mla_branch(z_mla, pos, past_lat, past_rope, past_pos, p['mla_q_norm_g'], p['mla_w_uq'],
                                  p['mla_kv_norm_g'], p['mla_w_uk'], p['mla_w_uv'], p['mla_out'])
    gates = jax.nn.sigmoid(z_gate.reshape(B, L, N_BRANCH, D_MODEL))
    merged = gates[:, :, 0] * br_a + gates[:, :, 1] * br_b + gates[:, :, 2] * br_c + gates[:, :, 3] * br_d
    x = x + rmsnorm(merged @ p['w_o'], p['g_post_mix'])
    hf = rmsnorm(x, p['g_pre_ffn'])
    f = jnp.square(jax.nn.relu(hf @ p['ffn_w1'])) @ p['ffn_w2']
    x = x + rmsnorm(f, p['g_post_ffn'])
    return x, conv_buf, gla_S, lat, krope, v_rows


def setup_inputs(seed: int = 0) -> dict:
    key = jax.random.key(seed)
    ks = iter(jax.random.split(key, 48))
    nrm = lambda shape, s=1.0: jax.random.normal(next(ks), shape, F32) * s
    gain = lambda shape: 1.0 + 0.05 * jax.random.normal(next(ks), shape, F32)
    n_pages = PAST_LEN // PAGE_SIZE
    n_used = DEC_BATCH * n_pages
    n_pool = n_used + n_used // 4
    inp = {}
    inp['x_prompt'] = nrm((BATCH, SEQ, D_MODEL))
    inp['x_sample'] = nrm((DEC_BATCH, DEC_SEQ, D_MODEL))
    inp['cache_mla_latent'] = nrm((DEPTH, n_pool, PAGE_SIZE, MLA_KV_RANK))
    inp['cache_mla_krope'] = nrm((DEPTH, n_pool, PAGE_SIZE, MLA_ROPE))
    inp['state_conv'] = nrm((DEPTH, DEC_BATCH, CONV_WIDTH - 1, CONV_DIM), 0.5)
    inp['state_gla'] = nrm((DEPTH, DEC_BATCH, GLA_HEADS, GLA_DK, GLA_DV), 0.5)
    inp['page_table'] = jax.random.permutation(next(ks), n_pool)[:n_used].reshape(DEC_BATCH, n_pages).astype(jnp.int32)
    inp['g_pre_mix'] = gain((DEPTH, D_MODEL))
    inp['w_in'] = nrm((DEPTH, D_MODEL, IN_TOTAL), D_MODEL ** -0.5)
    inp['conv_w'] = nrm((DEPTH, CONV_WIDTH, CONV_DIM), CONV_WIDTH ** -0.5)
    inp['conv_b'] = nrm((DEPTH, CONV_DIM), 0.02)
    inp['conv_ln_g'] = gain((DEPTH, CONV_DIM))
    inp['conv_ln_b'] = nrm((DEPTH, CONV_DIM), 0.02)
    inp['conv_out'] = nrm((DEPTH, CONV_DIM, D_MODEL), CONV_DIM ** -0.5)
    inp['gla_gate_up'] = nrm((DEPTH, GLA_GATE_RANK, GLA_HEADS * GLA_DK), GLA_GATE_RANK ** -0.5)
    inp['gla_gate_b'] = 1.0 + nrm((DEPTH, GLA_HEADS * GLA_DK), 0.5)
    inp['gla_norm_g'] = gain((DEPTH, GLA_HEADS, GLA_DV))
    inp['gla_out'] = nrm((DEPTH, GLA_HEADS * GLA_DV, D_MODEL), (GLA_HEADS * GLA_DV) ** -0.5)
    inp['gmlp_ln_g'] = gain((DEPTH, GMLP_DIM))
    inp['gmlp_ln_b'] = nrm((DEPTH, GMLP_DIM), 0.02)
    inp['gmlp_ws'] = nrm((DEPTH, GMLP_GROUPS, GMLP_CHUNK, GMLP_CHUNK), 0.5 * GMLP_CHUNK ** -0.5)
    inp['gmlp_bs'] = 1.0 + nrm((DEPTH, GMLP_GROUPS, GMLP_CHUNK), 0.1)
    inp['gmlp_out'] = nrm((DEPTH, GMLP_DIM, D_MODEL), GMLP_DIM ** -0.5)
    inp['mla_q_norm_g'] = gain((DEPTH, MLA_Q_RANK))
    inp['mla_w_uq'] = nrm((DEPTH, MLA_Q_RANK, MLA_HEADS * (MLA_NOPE + MLA_ROPE)), MLA_Q_RANK ** -0.5)
    inp['mla_kv_norm_g'] = gain((DEPTH, MLA_KV_RANK))
    inp['mla_w_uk'] = nrm((DEPTH, MLA_KV_RANK, MLA_HEADS, MLA_NOPE), MLA_KV_RANK ** -0.5)
    inp['mla_w_uv'] = nrm((DEPTH, MLA_KV_RANK, MLA_HEADS, MLA_V), MLA_KV_RANK ** -0.5)
    inp['mla_out'] = nrm((DEPTH, MLA_HEADS * MLA_V, D_MODEL), (MLA_HEADS * MLA_V) ** -0.5)
    inp['w_o'] = nrm((DEPTH, D_MODEL, D_MODEL), D_MODEL ** -0.5)
    inp['g_post_mix'] = gain((DEPTH, D_MODEL))
    inp['g_pre_ffn'] = gain((DEPTH, D_MODEL))
    inp['ffn_w1'] = nrm((DEPTH, D_MODEL, D_FF), D_MODEL ** -0.5)
    inp['ffn_w2'] = nrm((DEPTH, D_FF, D_MODEL), D_FF ** -0.5)
    inp['g_post_ffn'] = gain((DEPTH, D_MODEL))
    return inp


def reference(x_prompt, x_sample, cache_mla_latent, cache_mla_krope, state_conv, state_gla, page_table,
              g_pre_mix, w_in, conv_w, conv_b, conv_ln_g, conv_ln_b, conv_out,
              gla_gate_up, gla_gate_b, gla_norm_g, gla_out,
              gmlp_ln_g, gmlp_ln_b, gmlp_ws, gmlp_bs, gmlp_out,
              mla_q_norm_g, mla_w_uq, mla_kv_norm_g, mla_w_uk, mla_w_uv, mla_out,
              w_o, g_post_mix, g_pre_ffn, ffn_w1, ffn_w2, g_post_ffn):
    B, L = x_prompt.shape[0], x_prompt.shape[1]
    DB, LS = x_sample.shape[0], x_sample.shape[1]
    past_len = page_table.shape[1] * cache_mla_latent.shape[2]
    pos_p = jnp.arange(L, dtype=jnp.int32)
    pos_s = past_len + jnp.arange(LS, dtype=jnp.int32)
    past_pos = jnp.arange(past_len, dtype=jnp.int32)
    params = {'g_pre_mix': g_pre_mix, 'w_in': w_in, 'conv_w': conv_w, 'conv_b': conv_b, 'conv_ln_g': conv_ln_g,
              'conv_ln_b': conv_ln_b, 'conv_out': conv_out, 'gla_gate_up': gla_gate_up, 'gla_gate_b': gla_gate_b,
              'gla_norm_g': gla_norm_g, 'gla_out': gla_out, 'gmlp_ln_g': gmlp_ln_g, 'gmlp_ln_b': gmlp_ln_b,
              'gmlp_ws': gmlp_ws, 'gmlp_bs': gmlp_bs, 'gmlp_out': gmlp_out, 'mla_q_norm_g': mla_q_norm_g,
              'mla_w_uq': mla_w_uq, 'mla_kv_norm_g': mla_kv_norm_g, 'mla_w_uk': mla_w_uk, 'mla_w_uv': mla_w_uv,
              'mla_out': mla_out, 'w_o': w_o, 'g_post_mix': g_post_mix, 'g_pre_ffn': g_pre_ffn,
              'ffn_w1': ffn_w1, 'ffn_w2': ffn_w2, 'g_post_ffn': g_post_ffn}
    yp, ys = x_prompt, x_sample
    lat_p, kr_p, conv_p, gla_p = [], [], [], []
    lat_s, kr_s, conv_s, gla_s, gv_s = [], [], [], [], []
    for i in range(DEPTH):
        p = {name: w[i] for name, w in params.items()}
        conv0 = jnp.zeros((B, CONV_WIDTH - 1, CONV_DIM), yp.dtype)
        gla0 = jnp.zeros((B, GLA_HEADS, GLA_DK, GLA_DV), yp.dtype)
        yp, cb, S, lat, kr, _ = trunk_layer(yp, pos_p, conv0, gla0, None, None, None, p)
        lat_p.append(lat); kr_p.append(kr); conv_p.append(cb); gla_p.append(S)
        past_lat = cache_mla_latent[i][page_table].reshape(DB, past_len, MLA_KV_RANK)
        past_rope = cache_mla_krope[i][page_table].reshape(DB, past_len, MLA_ROPE)
        ys, cb, S, lat, kr, vr = trunk_layer(ys, pos_s, state_conv[i], state_gla[i], past_lat, past_rope, past_pos, p)
        lat_s.append(lat); kr_s.append(kr); conv_s.append(cb); gla_s.append(S); gv_s.append(vr)
    return (yp, ys,
            jnp.stack(lat_p), jnp.stack(kr_p), jnp.stack(conv_p), jnp.stack(gla_p),
            jnp.stack(lat_s), jnp.stack(kr_s), jnp.stack(conv_s), jnp.stack(gla_s), jnp.stack(gv_s))
```

```python
import functools
import math

import numpy as np
import jax
import jax.numpy as jnp
from jax import lax
from jax.experimental import pallas as pl
from jax.experimental.pallas import tpu as pltpu

F32 = jnp.float32
BF16 = jnp.bfloat16
EPS = 1e-6

D_MODEL = 1024
CONV_DIM = 512
CONV_WIDTH = 31
CONV_HALO = 32
GLA_HEADS = 4
GLA_DK = 64
GLA_DV = 128
GLA_GATE_RANK = 16
GLA_TAU = 16.0
GLA_CHUNK = 64
GLA_DIAG = 8
GMLP_DIM = 512
GMLP_GROUPS = 4
GMLP_CHUNK = 128
MLA_HEADS = 8
MLA_Q_RANK = 384
MLA_KV_RANK = 256
MLA_NOPE = 64
MLA_ROPE = 32
MLA_V = 64
MLA_SCALE = (MLA_NOPE + MLA_ROPE) ** -0.5
ROPE_THETA = 10000.0
D_FF = 4 * D_MODEL
LANE = 128
VMEM_LIMIT = 56 * 1024 * 1024

GLA_QK = GLA_HEADS * GLA_DK
GLA_V = GLA_HEADS * GLA_DV
Z_GLA = 2 * GLA_QK + 2 * GLA_V + LANE
Z_MLA = MLA_Q_RANK + MLA_KV_RANK + 2 * LANE


def _params(sem):
    return pltpu.CompilerParams(dimension_semantics=sem, vmem_limit_bytes=VMEM_LIMIT)


def _full(shape):
    n = len(shape)
    return pl.BlockSpec(shape, lambda *_: (0,) * n)


def _rms(x, g):
    return x * lax.rsqrt(jnp.mean(x * x, axis=-1, keepdims=True) + EPS) * g


def _ln(y, g, b):
    mu = jnp.mean(y, axis=-1, keepdims=True)
    yc = y - mu
    var = jnp.mean(yc * yc, axis=-1, keepdims=True)
    return yc * lax.rsqrt(var + EPS) * g + b


def _dot(a, b):
    return jnp.dot(a, b, preferred_element_type=F32)


def _dot_nt(a, b):
    return lax.dot_general(a, b, (((1,), (1,)), ((), ())), preferred_element_type=F32)


def _dot_tn(a, b):
    return lax.dot_general(a, b, (((0,), (0,)), ((), ())), preferred_element_type=F32)


def _inproj_kernel(x_ref, g_ref, w0, w1, w2, w3, o0, o1, o2, o3):
    h = _rms(x_ref[...], g_ref[...]).astype(BF16)
    for w, o in ((w0, o0), (w1, o1), (w2, o2), (w3, o3)):
        o[...] = _dot(h, w[...])


def _inproj(x, g, ws, tm):
    m = x.shape[0]
    return pl.pallas_call(
        _inproj_kernel,
        grid=(m // tm,),
        in_specs=[pl.BlockSpec((tm, D_MODEL), lambda i: (i, 0)), _full((1, D_MODEL))]
        + [_full(w.shape) for w in ws],
        out_specs=[pl.BlockSpec((tm, w.shape[1]), lambda i: (i, 0)) for w in ws],
        out_shape=[jax.ShapeDtypeStruct((m, w.shape[1]), F32) for w in ws],
        compiler_params=_params(("parallel",)),
        name="inproj",
    )(x, g, *ws)


def _conv_post(acc, lg, lb):
    y = _ln(acc, lg, lb)
    return (y * jax.nn.sigmoid(y)).astype(BF16)


def _conv_kernel(z_ref, cw_ref, cb_ref, lg_ref, lb_ref, act_ref, st_ref, full_sc, *, tl, sub):
    l = pl.program_id(1)

    @pl.when(l == 0)
    def _():
        full_sc[0:CONV_HALO, :] = jnp.zeros((CONV_HALO, CONV_DIM), F32)

    z = z_ref[...]
    full_sc[CONV_HALO:CONV_HALO + tl, :] = z[:, :CONV_DIM] * jax.nn.sigmoid(z[:, CONV_DIM:])
    first = CONV_HALO - (CONV_WIDTH - 1)
    for sb in range(tl // sub):
        acc = jnp.zeros((sub, CONV_DIM), F32) + cb_ref[...]
        for w in range(CONV_WIDTH):
            acc = acc + full_sc[sb * sub + first + w:sb * sub + first + w + sub, :] * cw_ref[w:w + 1, :]
        act_ref[sb * sub:(sb + 1) * sub, :] = _conv_post(acc, lg_ref[...], lb_ref[...])

    @pl.when(l == pl.num_programs(1) - 1)
    def _():
        st_ref[...] = full_sc[tl + first:tl + CONV_HALO, :]

    full_sc[0:CONV_HALO, :] = full_sc[tl:tl + CONV_HALO, :]


def _conv_prompt(z, cw, cb, lg, lb, nb, sl):
    tl = min(512, sl)
    sub = min(64, tl)
    nl = sl // tl
    vec = _full((1, CONV_DIM))
    return pl.pallas_call(
        functools.partial(_conv_kernel, tl=tl, sub=sub),
        grid=(nb, nl),
        in_specs=[pl.BlockSpec((tl, 2 * CONV_DIM), lambda b, l: (b * nl + l, 0)),
                  _full((CONV_WIDTH, CONV_DIM)), vec, vec, vec],
        out_specs=[pl.BlockSpec((tl, CONV_DIM), lambda b, l: (b * nl + l, 0)),
                   pl.BlockSpec((None, CONV_WIDTH - 1, CONV_DIM), lambda b, l: (b, 0, 0))],
        out_shape=[jax.ShapeDtypeStruct((nb * sl, CONV_DIM), BF16),
                   jax.ShapeDtypeStruct((nb, CONV_WIDTH - 1, CONV_DIM), F32)],
        scratch_shapes=[pltpu.VMEM((tl + CONV_HALO, CONV_DIM), F32)],
        compiler_params=_params(("parallel", "arbitrary")),
        name="conv_prompt",
    )(z, cw, cb, lg, lb)


def _conv_step_kernel(buf_ref, z_ref, cw_ref, cb_ref, lg_ref, lb_ref, act_ref, glu_ref, *, ls):
    nbuf = CONV_WIDTH - 1
    glu = []
    for t in range(ls):
        z = z_ref[t]
        glu.append(z[:, :CONV_DIM] * jax.nn.sigmoid(z[:, CONV_DIM:]))
        glu_ref[t] = glu[t]
    for t in range(ls):
        acc = jnp.zeros(glu[0].shape, F32) + cb_ref[...]
        for w in range(CONV_WIDTH):
            j = t + w
            src = buf_ref[j] if j < nbuf else glu[j - nbuf]
            acc = acc + src * cw_ref[w:w + 1, :]
        act_ref[t] = _conv_post(acc, lg_ref[...], lb_ref[...])


def _conv_step(buf_t, z_t, cw, cb, lg, lb):
    ls, nseq, _ = z_t.shape
    nb = min(32, nseq)
    vec = _full((1, CONV_DIM))
    return pl.pallas_call(
        functools.partial(_conv_step_kernel, ls=ls),
        grid=(nseq // nb,),
        in_specs=[pl.BlockSpec((CONV_WIDTH - 1, nb, CONV_DIM), lambda i: (0, i, 0)),
                  pl.BlockSpec((ls, nb, 2 * CONV_DIM), lambda i: (0, i, 0)),
                  _full((CONV_WIDTH, CONV_DIM)), vec, vec, vec],
        out_specs=[pl.BlockSpec((ls, nb, CONV_DIM), lambda i: (0, i, 0)),
                   pl.BlockSpec((ls, nb, CONV_DIM), lambda i: (0, i, 0))],
        out_shape=[jax.ShapeDtypeStruct((ls, nseq, CONV_DIM), BF16),
                   jax.ShapeDtypeStruct((ls, nseq, CONV_DIM), F32)],
        compiler_params=_params(("parallel",)),
        name="conv_step",
    )(buf_t, z_t, cw, cb, lg, lb)


def _gla_levels(c, diag):
    levels, s = [], c // 2
    while s >= diag:
        levels.append(s)
        s //= 2
    return tuple(levels)


def _gla_sum_masks(c, levels):
    i = np.arange(c)[:, None]
    r = np.arange(c)[None, :]
    mats = [r <= i, r > i]
    for s in levels:
        ref = (i & ~(2 * s - 1)) + s - 1
        upper = (i & s) != 0
        mats.append(np.where(upper, (r > ref) & (r <= i), (r > i) & (r <= ref)))
    return np.concatenate(mats, axis=0).astype(np.float32)


def _gla_kernel(z_ref, s0_ref, mall_ref, wg_ref, bg_ref, ng_ref, hsum_ref, act_ref, sout_ref, st_sc,
                *, c, diag, levels, valid, mmdt):
    ci = pl.program_id(1)

    @pl.when(ci == 0)
    def _():
        st_sc[...] = s0_ref[...]

    z = z_ref[...]
    q = z[:, 0:GLA_QK] * (GLA_DK ** -0.5)
    k = z[:, GLA_QK:2 * GLA_QK]
    v = z[:, 2 * GLA_QK:2 * GLA_QK + GLA_V]
    r = z[:, 2 * GLA_QK + GLA_V:2 * GLA_QK + 2 * GLA_V]
    gd = z[:, 2 * GLA_QK + 2 * GLA_V:]
    x = _dot(gd.astype(mmdt), wg_ref[...].astype(mmdt)) + bg_ref[...]
    g = (jnp.minimum(x, 0.0) - jnp.log(1.0 + jnp.exp(-jnp.abs(x)))) * (1.0 / GLA_TAU)
    row = lax.broadcasted_iota(jnp.int32, (c, 1), 0)
    if valid < c:
        g = jnp.where(row < valid, g, 0.0)

    g1 = g.astype(BF16)
    r1 = g - g1.astype(F32)
    g2 = r1.astype(BF16)
    g3 = (r1 - g2.astype(F32)).astype(BF16)
    gcat = jnp.concatenate([g1, g2, g3], axis=1).astype(mmdt)
    e3 = _dot(mall_ref[...].astype(mmdt), gcat)
    e = e3[:, 0:GLA_QK] + e3[:, GLA_QK:2 * GLA_QK] + e3[:, 2 * GLA_QK:]
    b = e[0:c]
    qe = (q * jnp.exp(b)).astype(mmdt)
    kd = (k * jnp.exp(e[c:2 * c])).astype(mmdt)
    dlast = jnp.exp(b[c - 1:c, :])
    vm = v.astype(mmdt)

    lvl = []
    if levels:
        ri = lax.broadcasted_iota(jnp.int32, (c, c), 0)
        rj = lax.broadcasted_iota(jnp.int32, (c, c), 1)
        for li, s in enumerate(levels):
            ee = jnp.exp(e[(2 + li) * c:(3 + li) * c])
            pair = ~(2 * s - 1)
            msk = ((ri & s) != 0) & ((rj & s) == 0) & ((ri & pair) == (rj & pair))
            lvl.append(((q * ee).astype(mmdt), (k * ee).astype(mmdt), msk))

    outs = []
    for h in range(GLA_HEADS):
        ks = slice(h * GLA_DK, (h + 1) * GLA_DK)
        vs = slice(h * GLA_DV, (h + 1) * GLA_DV)
        st = st_sc[h]
        o_h = _dot_nt(qe[:, ks], st.astype(mmdt))
        if lvl:
            a = jnp.zeros((c, c), F32)
            for ql, kl, msk in lvl:
                a = a + jnp.where(msk, _dot_nt(ql[:, ks], kl[:, ks]), 0.0)
            o_h = o_h + _dot(a.astype(mmdt), vm[:, vs])
        st_sc[h] = st * dlast[:, ks] + _dot_tn(vm[:, vs], kd[:, ks])
        outs.append(o_h)
    o = jnp.concatenate(outs, axis=1)

    hsum = hsum_ref[...].astype(mmdt)
    for d in range(min(diag, valid)):
        if d == 0:
            t = q * k
            vsh = v
        else:
            ok = (row & (diag - 1)) >= d
            dec = jnp.exp(jnp.where(ok, b - pltpu.roll(b, d, 0), 0.0))
            t = jnp.where(ok, q * pltpu.roll(k, d, 0) * dec, 0.0)
            vsh = pltpu.roll(v, d, 0)
        o = o + _dot(t.astype(mmdt), hsum) * vsh

    ng = ng_ref[...]
    normed = []
    for h in range(GLA_HEADS):
        vs = slice(h * GLA_DV, (h + 1) * GLA_DV)
        normed.append(_rms(o[:, vs], ng[:, vs]))
    act_ref[...] = (jnp.concatenate(normed, axis=1) * (r * jax.nn.sigmoid(r))).astype(act_ref.dtype)

    @pl.when(ci == pl.num_programs(1) - 1)
    def _():
        sout_ref[...] = st_sc[...]


def _gla(z, s0t, wg, bg, ng, nb, sl, c, valid, act_dtype):
    diag = min(GLA_DIAG, c)
    levels = _gla_levels(c, diag)
    mall = jnp.asarray(_gla_sum_masks(c, levels), BF16)
    hs = np.zeros((GLA_QK, GLA_V), np.float32)
    for h in range(GLA_HEADS):
        hs[h * GLA_DK:(h + 1) * GLA_DK, h * GLA_DV:(h + 1) * GLA_DV] = 1.0
    hsum = jnp.asarray(hs, BF16)
    nc = sl // c
    mmdt = BF16 if c % 16 == 0 else F32
    st_spec = pl.BlockSpec((None, GLA_HEADS, GLA_DV, GLA_DK), lambda b, i: (b, 0, 0, 0))
    return pl.pallas_call(
        functools.partial(_gla_kernel, c=c, diag=diag, levels=levels, valid=valid, mmdt=mmdt),
        grid=(nb, nc),
        in_specs=[pl.BlockSpec((c, Z_GLA), lambda b, i: (b * nc + i, 0)), st_spec,
                  _full(mall.shape), _full(wg.shape), _full(bg.shape), _full(ng.shape), _full(hsum.shape)],
        out_specs=[pl.BlockSpec((c, GLA_V), lambda b, i: (b * nc + i, 0)), st_spec],
        out_shape=[jax.ShapeDtypeStruct((nb * sl, GLA_V), act_dtype),
                   jax.ShapeDtypeStruct((nb, GLA_HEADS, GLA_DV, GLA_DK), F32)],
        scratch_shapes=[pltpu.VMEM((GLA_HEADS, GLA_DV, GLA_DK), F32)],
        compiler_params=_params(("parallel", "arbitrary")),
        name="gla",
    )(z, s0t, mall, wg, bg, ng, hsum)


def _gmlp_kernel(z_ref, ws_ref, bsb_ref, lg_ref, lb_ref, act_ref, *v_refs, nchunk):
    ti = lax.broadcasted_iota(jnp.int32, (GMLP_CHUNK, GMLP_CHUNK), 0)
    tj = lax.broadcasted_iota(jnp.int32, (GMLP_CHUNK, GMLP_CHUNK), 1)
    wt = [jnp.where(ti >= tj, ws_ref[g], 0.0).astype(BF16) for g in range(GMLP_GROUPS)]
    gc = GMLP_DIM // GMLP_GROUPS
    for ch in range(nchunk):
        rows = slice(ch * GMLP_CHUNK, (ch + 1) * GMLP_CHUNK)
        z = z_ref[rows, :]
        ge = z * (0.5 * (1.0 + jnp.tanh(math.sqrt(2.0 / math.pi) * (z + 0.044715 * (z * z * z)))))
        v = _ln(ge[:, GMLP_DIM:], lg_ref[...], lb_ref[...])
        if v_refs:
            v_refs[0][rows, :] = v
        vb = v.astype(BF16)
        s = jnp.concatenate([_dot(wt[g], vb[:, g * gc:(g + 1) * gc]) for g in range(GMLP_GROUPS)], axis=1)
        act_ref[rows, :] = (ge[:, :GMLP_DIM] * (s + bsb_ref[...])).astype(BF16)


def _gmlp(z, ws, bsb, lg, lb, emit_v):
    m = z.shape[0]
    tm = min(512, m)
    blk = lambda n: pl.BlockSpec((tm, n), lambda i: (i, 0))
    vec = _full((1, GMLP_DIM))
    out_specs = [blk(GMLP_DIM)]
    out_shape = [jax.ShapeDtypeStruct((m, GMLP_DIM), BF16)]
    if emit_v:
        out_specs.append(blk(GMLP_DIM))
        out_shape.append(jax.ShapeDtypeStruct((m, GMLP_DIM), F32))
    return pl.pallas_call(
        functools.partial(_gmlp_kernel, nchunk=tm // GMLP_CHUNK),
        grid=(m // tm,),
        in_specs=[blk(2 * GMLP_DIM), _full(ws.shape), _full(bsb.shape), vec, vec],
        out_specs=out_specs,
        out_shape=out_shape,
        compiler_params=_params(("parallel",)),
        name="gmlp",
    )(z, ws, bsb, lg, lb)


def _mla_latents(z, ck, sk, gkv, lat_ref, kr_ref):
    c = _rms(z[:, MLA_Q_RANK:MLA_Q_RANK + MLA_KV_RANK], gkv)
    lat_ref[...] = c
    o = MLA_Q_RANK + MLA_KV_RANK
    kr = z[:, o:o + LANE] * ck + z[:, o + LANE:o + 2 * LANE] * sk
    kr_ref[...] = kr[:, :MLA_ROPE]
    return c, kr


def _mla_proj_kernel(z_ref, cq_ref, sq_ref, ck_ref, sk_ref, gq_ref, gkv_ref, wq1_ref, wq2_ref, wuk_ref,
                     wuv_ref, sel_ref, q_ref, k_ref, v_ref, lat_ref, kr_ref):
    z = z_ref[...]
    hq = _rms(z[:, :MLA_Q_RANK], gq_ref[...]).astype(BF16)
    q1 = _dot(hq, wq1_ref[...])
    q2 = _dot(hq, wq2_ref[...])
    cosq = cq_ref[...]
    sinq = sq_ref[...]
    for h in range(MLA_HEADS):
        hs = slice(h * LANE, (h + 1) * LANE)
        q_ref[:, hs] = ((q1[:, hs] * cosq + q2[:, hs] * sinq) * MLA_SCALE).astype(BF16)
    c, kr = _mla_latents(z, ck_ref[...], sk_ref[...], gkv_ref[...], lat_ref, kr_ref)
    cb = c.astype(BF16)
    k_ref[...] = (_dot(cb, wuk_ref[...]) + _dot(kr.astype(BF16), sel_ref[...])).astype(BF16)
    v_ref[...] = _dot(cb, wuv_ref[...]).astype(BF16)


def _mla_proj(z, tabs, gq, gkv, wq1, wq2, wuk, wuv, sel, sl):
    m = z.shape[0]
    tm = min(512, sl)
    nl = sl // tm
    row = lambda n: pl.BlockSpec((tm, n), lambda i: (i, 0))
    tab = pl.BlockSpec((tm, LANE), lambda i: (i % nl, 0))
    hd = MLA_HEADS * LANE
    return pl.pallas_call(
        _mla_proj_kernel,
        grid=(m // tm,),
        in_specs=[row(Z_MLA), tab, tab, tab, tab, _full(gq.shape), _full(gkv.shape), _full(wq1.shape),
                  _full(wq2.shape), _full(wuk.shape), _full(wuv.shape), _full(sel.shape)],
        out_specs=[row(hd), row(hd), row(MLA_HEADS * MLA_V), row(MLA_KV_RANK), row(MLA_ROPE)],
        out_shape=[jax.ShapeDtypeStruct((m, hd), BF16), jax.ShapeDtypeStruct((m, hd), BF16),
                   jax.ShapeDtypeStruct((m, MLA_HEADS * MLA_V), BF16),
                   jax.ShapeDtypeStruct((m, MLA_KV_RANK), F32), jax.ShapeDtypeStruct((m, MLA_ROPE), F32)],
        compiler_params=_params(("parallel",)),
        name="mla_proj",
    )(z, *tabs, gq, gkv, wq1, wq2, wuk, wuv, sel)


def _flash_kernel(q_ref, k_ref, v_ref, o_ref, m_sc, l_sc, acc_sc, *, t):
    qi = pl.program_id(2)
    ki = pl.program_id(3)

    @pl.when(ki == 0)
    def _():
        m_sc[...] = jnp.full(m_sc.shape, -jnp.inf, F32)
        l_sc[...] = jnp.zeros(l_sc.shape, F32)
        acc_sc[...] = jnp.zeros(acc_sc.shape, F32)

    def update(masked):
        for a in range(2):
            hs = slice(a * LANE, (a + 1) * LANE)
            s = _dot_nt(q_ref[:, hs], k_ref[:, hs])
            if masked:
                ri = lax.broadcasted_iota(jnp.int32, (t, t), 0)
                rj = lax.broadcasted_iota(jnp.int32, (t, t), 1)
                s = jnp.where(ri >= rj, s, -jnp.inf)
            m_prev = m_sc[a]
            m_new = jnp.maximum(m_prev, jnp.max(s, axis=-1, keepdims=True))
            alpha = jnp.exp(m_prev - m_new)
            p = jnp.exp(s - m_new)
            l_sc[a] = alpha * l_sc[a] + jnp.sum(p, axis=-1, keepdims=True)
            acc_sc[a] = alpha * acc_sc[a] + _dot(p.astype(BF16), v_ref[...])
            m_sc[a] = m_new

    @pl.when(ki < qi)
    def _():
        update(False)

    @pl.when(ki == qi)
    def _():
        update(True)
        lane = lax.broadcasted_iota(jnp.int32, (t, LANE), 1)
        o_ref[...] = jnp.where(lane < MLA_V, acc_sc[0] / l_sc[0], acc_sc[1] / l_sc[1]).astype(BF16)


def _flash(q, k, v, nb, sl):
    t = min(512, sl)
    nt = sl // t
    qk = lambda f: pl.BlockSpec((t, 2 * LANE), f)
    return pl.pallas_call(
        functools.partial(_flash_kernel, t=t),
        grid=(nb, MLA_HEADS // 2, nt, nt),
        in_specs=[qk(lambda b, h, i, j: (b * nt + i, h)),
                  qk(lambda b, h, i, j: (b * nt + jnp.minimum(i, j), h)),
                  pl.BlockSpec((t, LANE), lambda b, h, i, j: (b * nt + jnp.minimum(i, j), h))],
        out_specs=pl.BlockSpec((t, LANE), lambda b, h, i, j: (b * nt + i, h)),
        out_shape=jax.ShapeDtypeStruct((nb * sl, MLA_HEADS * MLA_V), BF16),
        scratch_shapes=[pltpu.VMEM((2, t, 1), F32), pltpu.VMEM((2, t, 1), F32), pltpu.VMEM((2, t, LANE), F32)],
        compiler_params=_params(("parallel", "parallel", "parallel", "arbitrary")),
        name="mla_flash",
    )(q, k, v)


def _mla_proj_step_kernel(z_ref, c8_ref, s8_ref, ck_ref, sk_ref, gq_ref, gkv_ref, wqn_ref, wqr_ref, wqs_ref,
                          wukt_ref, qlat_ref, qr_ref, lat_ref, kr_ref):
    z = z_ref[...]
    hq = _rms(z[:, :MLA_Q_RANK], gq_ref[...]).astype(BF16)
    qn = _dot(hq, wqn_ref[...]).astype(BF16)
    for h in range(MLA_HEADS):
        lat = _dot(qn[:, h * LANE:(h + 1) * LANE], wukt_ref[h])
        qlat_ref[:, h * MLA_KV_RANK:(h + 1) * MLA_KV_RANK] = (lat * MLA_SCALE).astype(BF16)
    qr = _dot(hq, wqr_ref[...]) * c8_ref[...] + _dot(hq, wqs_ref[...]) * s8_ref[...]
    qr_ref[...] = (qr * MLA_SCALE).astype(BF16)
    _mla_latents(z, ck_ref[...], sk_ref[...], gkv_ref[...], lat_ref, kr_ref)


def _mla_proj_step(z, tabs, gq, gkv, wqn, wqr, wqs, wukt):
    m = z.shape[0]
    args = (z, *tabs, gq, gkv, wqn, wqr, wqs, wukt)
    return pl.pallas_call(
        _mla_proj_step_kernel,
        grid=(1,),
        in_specs=[_full(a.shape) for a in args],
        out_specs=[_full((m, MLA_HEADS * MLA_KV_RANK)), _full((m, MLA_HEADS * MLA_ROPE)),
                   _full((m, MLA_KV_RANK)), _full((m, MLA_ROPE))],
        out_shape=[jax.ShapeDtypeStruct((m, MLA_HEADS * MLA_KV_RANK), BF16),
                   jax.ShapeDtypeStruct((m, MLA_HEADS * MLA_ROPE), BF16),
                   jax.ShapeDtypeStruct((m, MLA_KV_RANK), F32), jax.ShapeDtypeStruct((m, MLA_ROPE), F32)],
        compiler_params=_params(("arbitrary",)),
        name="mla_proj_step",
    )(*args)


def _paged_kernel(pt_ref, ql_ref, qr_ref, cn_ref, rn_ref, *rest, pp, ls, npad):
    del pt_ref
    lat_refs = rest[:pp]
    kr_refs = rest[pp:2 * pp]
    o_ref, m_sc, l_sc, acc_sc = rest[2 * pp:]
    j = pl.program_id(1)
    ql = ql_ref[...]
    qr = qr_ref[...]
    nrow = ql.shape[0]

    @pl.when(j == 0)
    def _():
        cn = cn_ref[...]
        s = _dot_nt(ql.astype(F32), cn) + _dot_nt(qr.astype(F32), rn_ref[...])
        tok = lax.broadcasted_iota(jnp.int32, (nrow, npad), 0) // MLA_HEADS
        key = lax.broadcasted_iota(jnp.int32, (nrow, npad), 1)
        s = jnp.where((key <= tok) & (key < ls), s, -jnp.inf)
        m = jnp.max(s, axis=-1, keepdims=True)
        p = jnp.exp(s - m)
        m_sc[...] = m
        l_sc[...] = jnp.sum(p, axis=-1, keepdims=True)
        acc_sc[...] = _dot(p, cn)

    kb = [lat_refs[i][...].astype(BF16) for i in range(pp)]
    s = jnp.concatenate(
        [_dot_nt(ql, kb[i]) + _dot_nt(qr, kr_refs[i][...].astype(BF16)) for i in range(pp)], axis=1)
    m_prev = m_sc[...]
    m_new = jnp.maximum(m_prev, jnp.max(s, axis=-1, keepdims=True))
    alpha = jnp.exp(m_prev - m_new)
    p = jnp.exp(s - m_new)
    l_sc[...] = alpha * l_sc[...] + jnp.sum(p, axis=-1, keepdims=True)
    pb = p.astype(BF16)
    pv = _dot(pb[:, 0:LANE], kb[0])
    for i in range(1, pp):
        pv = pv + _dot(pb[:, i * LANE:(i + 1) * LANE], kb[i])
    acc_sc[...] = alpha * acc_sc[...] + pv
    m_sc[...] = m_new

    @pl.when(j == pl.num_programs(1) - 1)
    def _():
        o_ref[...] = (acc_sc[...] / l_sc[...]).astype(BF16)


def _paged_attention(pt_flat, ql, qr, cn, rn, cache_lat, cache_kr, layer, n_pages, ls):
    nseq, nrow, _ = ql.shape
    page = cache_lat.shape[2]
    assert page == LANE
    pp = min(16, n_pages)
    npad = cn.shape[1]
    seq = lambda shp: pl.BlockSpec((None,) + shp, lambda b, j, pt: (b, 0, 0))

    def page_spec(width, i):
        return pl.BlockSpec((None, None, page, width),
                            lambda b, j, pt: (layer, pt[b * n_pages + j * pp + i], 0, 0))

    grid_spec = pltpu.PrefetchScalarGridSpec(
        num_scalar_prefetch=1,
        grid=(nseq, n_pages // pp),
        in_specs=[seq((nrow, MLA_KV_RANK)), seq((nrow, MLA_ROPE)), seq((npad, MLA_KV_RANK)),
                  seq((npad, MLA_ROPE))]
        + [page_spec(MLA_KV_RANK, i) for i in range(pp)] + [page_spec(MLA_ROPE, i) for i in range(pp)],
        out_specs=seq((nrow, MLA_KV_RANK)),
        scratch_shapes=[pltpu.VMEM((nrow, 1), F32), pltpu.VMEM((nrow, 1), F32),
                        pltpu.VMEM((nrow, MLA_KV_RANK), F32)],
    )
    return pl.pallas_call(
        functools.partial(_paged_kernel, pp=pp, ls=ls, npad=npad),
        grid_spec=grid_spec,
        out_shape=jax.ShapeDtypeStruct((nseq, nrow, MLA_KV_RANK), BF16),
        compiler_params=_params(("parallel", "arbitrary")),
        name="mla_paged",
    )(pt_flat, ql, qr, cn, rn, *([cache_lat] * pp), *([cache_kr] * pp))


def _mla_up_kernel(o_ref, w_ref, out_ref):
    acc = _dot(o_ref[:, 0:MLA_KV_RANK], w_ref[0])
    for h in range(1, MLA_HEADS):
        acc = acc + _dot(o_ref[:, h * MLA_KV_RANK:(h + 1) * MLA_KV_RANK], w_ref[h])
    out_ref[...] = acc.astype(BF16)


def _mla_up(o_lat, wuvp):
    m = o_lat.shape[0]
    return pl.pallas_call(
        _mla_up_kernel,
        grid=(1,),
        in_specs=[_full(o_lat.shape), _full(wuvp.shape)],
        out_specs=_full((m, MLA_HEADS * MLA_V)),
        out_shape=jax.ShapeDtypeStruct((m, MLA_HEADS * MLA_V), BF16),
        compiler_params=_params(("arbitrary",)),
        name="mla_up",
    )(o_lat, wuvp)


def _merge_kernel(x_ref, a_ref, b_ref, c_ref, d_ref, g_ref, wg_ref, wa_ref, wb_ref, wc_ref, wd_ref, wo_ref,
                  gp_ref, o_ref):
    x = x_ref[...]
    h = _rms(x, g_ref[...]).astype(BF16)
    merged = None
    for i, (act, w) in enumerate(((a_ref, wa_ref), (b_ref, wb_ref), (c_ref, wc_ref), (d_ref, wd_ref))):
        gate = jax.nn.sigmoid(_dot(h, wg_ref[:, i * D_MODEL:(i + 1) * D_MODEL]))
        term = gate * _dot(act[...], w[...])
        merged = term if merged is None else merged + term
    y = _dot(merged.astype(BF16), wo_ref[...])
    o_ref[...] = x + _rms(y, gp_ref[...])


def _merge(x, acts, g, wgate, wouts, wo, gp):
    m = x.shape[0]
    tm = min(256, m)
    row = lambda n: pl.BlockSpec((tm, n), lambda i: (i, 0))
    vec = _full((1, D_MODEL))
    return pl.pallas_call(
        _merge_kernel,
        grid=(m // tm,),
        in_specs=[row(D_MODEL)] + [row(a.shape[1]) for a in acts] + [vec, _full(wgate.shape)]
        + [_full(w.shape) for w in wouts] + [_full(wo.shape), vec],
        out_specs=row(D_MODEL),
        out_shape=jax.ShapeDtypeStruct((m, D_MODEL), F32),
        compiler_params=_params(("parallel",)),
        name="merge",
    )(x, *acts, g, wgate, *wouts, wo, gp)


def _ffn_kernel(x_ref, g_ref, w1_ref, w2_ref, gp_ref, o_ref, h_sc, acc_sc):
    j = pl.program_id(1)

    @pl.when(j == 0)
    def _():
        h_sc[...] = _rms(x_ref[...], g_ref[...]).astype(BF16)
        acc_sc[...] = jnp.zeros(acc_sc.shape, F32)

    t = jnp.maximum(_dot(h_sc[...], w1_ref[...]), 0.0)
    acc_sc[...] += _dot((t * t).astype(BF16), w2_ref[...])

    @pl.when(j == pl.num_programs(1) - 1)
    def _():
        o_ref[...] = x_ref[...] + _rms(acc_sc[...], gp_ref[...])


def _ffn(x, g, w1, w2, gp):
    m = x.shape[0]
    tm = min(1024, m)
    tf = 1024
    vec = _full((1, D_MODEL))
    return pl.pallas_call(
        _ffn_kernel,
        grid=(m // tm, D_FF // tf),
        in_specs=[pl.BlockSpec((tm, D_MODEL), lambda i, j: (i, 0)), vec,
                  pl.BlockSpec((D_MODEL, tf), lambda i, j: (0, j)),
                  pl.BlockSpec((tf, D_MODEL), lambda i, j: (j, 0)), vec],
        out_specs=pl.BlockSpec((tm, D_MODEL), lambda i, j: (i, 0)),
        out_shape=jax.ShapeDtypeStruct((m, D_MODEL), F32),
        scratch_shapes=[pltpu.VMEM((tm, D_MODEL), BF16), pltpu.VMEM((tm, D_MODEL), F32)],
        compiler_params=_params(("parallel", "arbitrary")),
        name="ffn",
    )(x, g, w1, w2, gp)


def _swap_halves(w):
    half = w.shape[-1] // 2
    return jnp.concatenate([w[..., half:], w[..., :half]], axis=-1)


def _prep_layer(p):
    w_in = p['w_in']
    o_gla = 2 * CONV_DIM
    o_gd = o_gla + 2 * GLA_QK + GLA_V
    o_r = o_gd + GLA_GATE_RANK
    o_gmlp = o_r + GLA_V
    o_mla = o_gmlp + 2 * GMLP_DIM
    o_kr = o_mla + MLA_Q_RANK + MLA_KV_RANK
    o_gate = o_kr + MLA_ROPE
    rows = w_in.shape[0]
    zpad = lambda n: jnp.zeros((rows, n), w_in.dtype)
    w_kr = w_in[:, o_kr:o_gate]
    q = {}
    q['w_conv'] = w_in[:, :o_gla].astype(BF16)
    q['w_gla'] = jnp.concatenate([w_in[:, o_gla:o_gd], w_in[:, o_r:o_gmlp], w_in[:, o_gd:o_r],
                                  zpad(LANE - GLA_GATE_RANK)], axis=1).astype(BF16)
    q['w_gmlp'] = w_in[:, o_gmlp:o_mla].astype(BF16)
    q['w_mla'] = jnp.concatenate([w_in[:, o_mla:o_kr], w_kr, zpad(LANE - MLA_ROPE), _swap_halves(w_kr),
                                  zpad(LANE - MLA_ROPE)], axis=1).astype(BF16)
    q['w_gate'] = w_in[:, o_gate:].astype(BF16)

    q['gla_wg'] = jnp.concatenate([p['gla_gate_up'], jnp.zeros((LANE - GLA_GATE_RANK, GLA_QK), F32)],
                                  axis=0).astype(BF16)
    q['gla_bg'] = p['gla_gate_b'].reshape(1, GLA_QK)
    q['gla_ng'] = p['gla_norm_g'].reshape(1, GLA_V)

    rq = MLA_Q_RANK
    wq = p['mla_w_uq'].reshape(rq, MLA_HEADS, MLA_NOPE + MLA_ROPE)
    nope, rope = wq[..., :MLA_NOPE], wq[..., MLA_NOPE:]
    rope_sw = _swap_halves(rope)
    zq = lambda n: jnp.zeros((rq, MLA_HEADS, n), F32)
    hd = MLA_HEADS * LANE
    q['wq1'] = jnp.concatenate([nope, rope, zq(LANE - MLA_NOPE - MLA_ROPE)], -1).reshape(rq, hd).astype(BF16)
    q['wq2'] = jnp.concatenate([zq(MLA_NOPE), rope_sw, zq(LANE - MLA_NOPE - MLA_ROPE)], -1).reshape(rq, hd).astype(BF16)
    q['wqn'] = jnp.concatenate([nope, zq(LANE - MLA_NOPE)], -1).reshape(rq, hd).astype(BF16)
    q['wqr'] = rope.reshape(rq, MLA_HEADS * MLA_ROPE).astype(BF16)
    q['wqs'] = rope_sw.reshape(rq, MLA_HEADS * MLA_ROPE).astype(BF16)
    wuk, wuv = p['mla_w_uk'], p['mla_w_uv']
    zk = jnp.zeros((MLA_KV_RANK, MLA_HEADS, LANE - MLA_NOPE), F32)
    q['wuk'] = jnp.concatenate([wuk, zk], -1).reshape(MLA_KV_RANK, hd).astype(BF16)
    q['wuv'] = wuv.reshape(MLA_KV_RANK, MLA_HEADS * MLA_V).astype(BF16)
    q['wukt'] = jnp.concatenate([wuk, zk], -1).transpose(1, 2, 0).astype(BF16)
    eye = jnp.eye(MLA_HEADS, dtype=F32)
    q['wuvp'] = (wuv.transpose(1, 0, 2)[:, :, None, :] * eye[:, None, :, None]).reshape(
        MLA_HEADS, MLA_KV_RANK, MLA_HEADS * MLA_V).astype(BF16)

    for name in ('conv_out', 'gla_out', 'gmlp_out', 'mla_out', 'w_o', 'ffn_w1', 'ffn_w2'):
        q[name] = p[name].astype(BF16)
    for name in ('g_pre_mix', 'g_post_mix', 'g_pre_ffn', 'g_post_ffn', 'conv_b', 'conv_ln_g', 'conv_ln_b',
                 'gmlp_ln_g', 'gmlp_ln_b', 'mla_q_norm_g', 'mla_kv_norm_g'):
        q[name] = p[name].reshape(1, -1)
    q['conv_w'] = p['conv_w']
    q['gmlp_ws'] = p['gmlp_ws']
    q['gmlp_bs'] = p['gmlp_bs']
    return q


def _rope_tables(pos):
    half = MLA_ROPE // 2
    inv = ROPE_THETA ** (-jnp.arange(half, dtype=F32) / half)
    ang = pos.astype(F32)[:, None] * inv[None, :]
    cos, sin = jnp.cos(ang), jnp.sin(ang)
    return jnp.concatenate([cos, cos], -1), jnp.concatenate([-sin, sin], -1)


def _key_sel():
    sel = np.zeros((LANE, MLA_HEADS * LANE), np.float32)
    for h in range(MLA_HEADS):
        for j in range(MLA_ROPE):
            sel[j, h * LANE + MLA_NOPE + j] = 1.0
    return jnp.asarray(sel, BF16)


def _pad_lanes(t, left, total):
    n = t.shape[0]
    return jnp.concatenate([jnp.zeros((n, left), F32), t, jnp.zeros((n, total - left - t.shape[1]), F32)], -1)


def _mix_tail(x, acts, q):
    x = _merge(x, acts, q['g_pre_mix'], q['w_gate'],
               (q['conv_out'], q['gla_out'], q['gmlp_out'], q['mla_out']), q['w_o'], q['g_post_mix'])
    return _ffn(x, q['g_pre_ffn'], q['ffn_w1'], q['ffn_w2'], q['g_post_ffn'])


def _layer_prompt(x, q, tabs, sel, nb, sl):
    m = nb * sl
    z_conv, z_gla, z_gmlp, z_mla = _inproj(x, q['g_pre_mix'], (q['w_conv'], q['w_gla'], q['w_gmlp'], q['w_mla']),
                                           min(256, m))
    act_a, conv_st = _conv_prompt(z_conv, q['conv_w'], q['conv_b'], q['conv_ln_g'], q['conv_ln_b'], nb, sl)
    c = min(GLA_CHUNK, sl)
    s0t = jnp.zeros((nb, GLA_HEADS, GLA_DV, GLA_DK), F32)
    act_b, st = _gla(z_gla, s0t, q['gla_wg'], q['gla_bg'], q['gla_ng'], nb, sl, c, c, BF16)
    bsb = jnp.repeat(q['gmlp_bs'].T, GMLP_DIM // GMLP_GROUPS, axis=1)
    (act_c,) = _gmlp(z_gmlp, q['gmlp_ws'], bsb, q['gmlp_ln_g'], q['gmlp_ln_b'], False)
    qp, kp, v, lat, kr = _mla_proj(z_mla, tabs, q['mla_q_norm_g'], q['mla_kv_norm_g'], q['wq1'], q['wq2'],
                                   q['wuk'], q['wuv'], sel, sl)
    act_d = _flash(qp, kp, v, nb, sl)
    x = _mix_tail(x, (act_a, act_b, act_c, act_d), q)
    return x, lat, kr, conv_st, st.swapaxes(-1, -2)


def _layer_sample(x, q, tabs, nseq, ls, conv_buf, gla_s, cache_lat, cache_kr, pt_flat, layer, n_pages):
    m = nseq * ls
    z_conv, z_gla, z_gmlp, z_mla = _inproj(x, q['g_pre_mix'], (q['w_conv'], q['w_gla'], q['w_gmlp'], q['w_mla']),
                                           min(256, m))
    act_t, glu_t = _conv_step(conv_buf.transpose(1, 0, 2), z_conv.reshape(nseq, ls, -1).transpose(1, 0, 2),
                              q['conv_w'], q['conv_b'], q['conv_ln_g'], q['conv_ln_b'])
    act_a = act_t.transpose(1, 0, 2).reshape(m, CONV_DIM)
    conv_st = jnp.concatenate([conv_buf, glu_t.transpose(1, 0, 2)], axis=1)[:, -(CONV_WIDTH - 1):]

    lp = 8
    pad = lambda z: jnp.pad(z.reshape(nseq, ls, -1), ((0, 0), (0, lp - ls), (0, 0))).reshape(nseq * lp, -1)
    unpad = lambda a: a.reshape(nseq, lp, -1)[:, :ls].reshape(m, -1)
    act_b, st = _gla(pad(z_gla), gla_s.swapaxes(-1, -2), q['gla_wg'], q['gla_bg'], q['gla_ng'], nseq, lp, lp, ls, F32)
    act_b = unpad(act_b).astype(BF16)

    per = GMLP_CHUNK // lp
    eye = jnp.eye(per, dtype=F32)
    wk = (eye[None, :, None, :, None] * q['gmlp_ws'][:, None, :lp, None, :lp]).reshape(
        GMLP_GROUPS, GMLP_CHUNK, GMLP_CHUNK)
    bsb = jnp.repeat(jnp.tile(q['gmlp_bs'][:, :lp], (1, per)).T, GMLP_DIM // GMLP_GROUPS, axis=1)
    act_c, v_rows = _gmlp(pad(z_gmlp), wk, bsb, q['gmlp_ln_g'], q['gmlp_ln_b'], True)
    act_c, v_rows = unpad(act_c), unpad(v_rows)

    ql, qr, lat, kr = _mla_proj_step(z_mla, tabs, q['mla_q_norm_g'], q['mla_kv_norm_g'], q['wqn'], q['wqr'],
                                     q['wqs'], q['wukt'])
    npad = 8
    padk = lambda t: jnp.pad(t.reshape(nseq, ls, -1), ((0, 0), (0, npad - ls), (0, 0)))
    o_lat = _paged_attention(pt_flat, ql.reshape(nseq, ls * MLA_HEADS, MLA_KV_RANK),
                             qr.reshape(nseq, ls * MLA_HEADS, MLA_ROPE), padk(lat), padk(kr),
                             cache_lat, cache_kr, layer, n_pages, ls)
    act_d = _mla_up(o_lat.reshape(m, MLA_HEADS * MLA_KV_RANK), q['wuvp'])
    x = _mix_tail(x, (act_a, act_b, act_c, act_d), q)
    return x, lat, kr, conv_st, st.swapaxes(-1, -2), v_rows


def kernel(x_prompt, x_sample, cache_mla_latent, cache_mla_krope, state_conv, state_gla, page_table,
           g_pre_mix, w_in, conv_w, conv_b, conv_ln_g, conv_ln_b, conv_out,
           gla_gate_up, gla_gate_b, gla_norm_g, gla_out,
           gmlp_ln_g, gmlp_ln_b, gmlp_ws, gmlp_bs, gmlp_out,
           mla_q_norm_g, mla_w_uq, mla_kv_norm_g, mla_w_uk, mla_w_uv, mla_out,
           w_o, g_post_mix, g_pre_ffn, ffn_w1, ffn_w2, g_post_ffn):
    params = {'g_pre_mix': g_pre_mix, 'w_in': w_in, 'conv_w': conv_w, 'conv_b': conv_b, 'conv_ln_g': conv_ln_g,
              'conv_ln_b': conv_ln_b, 'conv_out': conv_out, 'gla_gate_up': gla_gate_up, 'gla_gate_b': gla_gate_b,
              'gla_norm_g': gla_norm_g, 'gla_out': gla_out, 'gmlp_ln_g': gmlp_ln_g, 'gmlp_ln_b': gmlp_ln_b,
              'gmlp_ws': gmlp_ws, 'gmlp_bs': gmlp_bs, 'gmlp_out': gmlp_out, 'mla_q_norm_g': mla_q_norm_g,
              'mla_w_uq': mla_w_uq, 'mla_kv_norm_g': mla_kv_norm_g, 'mla_w_uk': mla_w_uk, 'mla_w_uv': mla_w_uv,
              'mla_out': mla_out, 'w_o': w_o, 'g_post_mix': g_post_mix, 'g_pre_ffn': g_pre_ffn,
              'ffn_w1': ffn_w1, 'ffn_w2': ffn_w2, 'g_post_ffn': g_post_ffn}
    depth = w_in.shape[0]
    nb, sl, _ = x_prompt.shape
    nseq, ls, _ = x_sample.shape
    n_pages = page_table.shape[1]
    past_len = n_pages * cache_mla_latent.shape[2]

    cos_p, sin_p = _rope_tables(jnp.arange(sl, dtype=jnp.int32))
    tabs_p = (jnp.concatenate([jnp.ones((sl, MLA_NOPE), F32), cos_p, jnp.zeros((sl, LANE - MLA_NOPE - MLA_ROPE), F32)], -1),
              _pad_lanes(sin_p, MLA_NOPE, LANE), _pad_lanes(cos_p, 0, LANE), _pad_lanes(sin_p, 0, LANE))
    cos_s, sin_s = _rope_tables(past_len + jnp.arange(ls, dtype=jnp.int32))
    cos_s, sin_s = jnp.tile(cos_s, (nseq, 1)), jnp.tile(sin_s, (nseq, 1))
    tabs_s = (jnp.tile(cos_s, (1, MLA_HEADS)), jnp.tile(sin_s, (1, MLA_HEADS)),
              _pad_lanes(cos_s, 0, LANE), _pad_lanes(sin_s, 0, LANE))
    sel = _key_sel()
    pt_flat = page_table.reshape(-1)

    xp = x_prompt.reshape(nb * sl, D_MODEL)
    xs = x_sample.reshape(nseq * ls, D_MODEL)
    outs_p, outs_s = [], []
    for i in range(depth):
        q = _prep_layer({name: w[i] for name, w in params.items()})
        xp, lat, kr, cst, st = _layer_prompt(xp, q, tabs_p, sel, nb, sl)
        outs_p.append((lat.reshape(nb, sl, -1), kr.reshape(nb, sl, -1), cst, st))
        xs, lat, kr, cst, st, vr = _layer_sample(xs, q, tabs_s, nseq, ls, state_conv[i], state_gla[i],
                                                 cache_mla_latent, cache_mla_krope, pt_flat, i, n_pages)
        outs_s.append((lat.reshape(nseq, ls, -1), kr.reshape(nseq, ls, -1), cst, st, vr.reshape(nseq, ls, -1)))
    stack = lambda outs, j: jnp.stack([o[j] for o in outs])
    return (xp.reshape(nb, sl, D_MODEL), xs.reshape(nseq, ls, D_MODEL),
            stack(outs_p, 0), stack(outs_p, 1), stack(outs_p, 2), stack(outs_p, 3),
            stack(outs_s, 0), stack(outs_s, 1), stack(outs_s, 2), stack(outs_s, 3), stack(outs_s, 4))
```

```python
import functools
import math

import numpy as np
import jax
import jax.numpy as jnp
from jax import lax
from jax.experimental import pallas as pl
from jax.experimental.pallas import tpu as pltpu

F32 = jnp.float32
BF16 = jnp.bfloat16
EPS = 1e-6

D_MODEL = 1024
CONV_DIM = 512
CONV_WIDTH = 31
CONV_HALO = 32
GLA_HEADS = 4
GLA_DK = 64
GLA_DV = 128
GLA_GATE_RANK = 16
GLA_TAU = 16.0
GLA_CHUNK = 64
GLA_DIAG = 8
GLA_SEQS_PROMPT = 4
GLA_SEQS_STEP = 8
GMLP_DIM = 512
GMLP_GROUPS = 4
GMLP_CHUNK = 128
MLA_HEADS = 8
MLA_Q_RANK = 384
MLA_KV_RANK = 256
MLA_NOPE = 64
MLA_ROPE = 32
MLA_V = 64
MLA_SCALE = (MLA_NOPE + MLA_ROPE) ** -0.5
ROPE_THETA = 10000.0
D_FF = 4 * D_MODEL
LANE = 128
SUBLANE = 8
VMEM_LIMIT = 56 * 1024 * 1024

GLA_QK = GLA_HEADS * GLA_DK
GLA_V = GLA_HEADS * GLA_DV
Z_GLA = 2 * GLA_QK + 2 * GLA_V + LANE
Z_MLA = MLA_Q_RANK + MLA_KV_RANK + 2 * LANE


def _params(sem):
    return pltpu.CompilerParams(dimension_semantics=sem, vmem_limit_bytes=VMEM_LIMIT)


def _full(shape):
    n = len(shape)
    return pl.BlockSpec(shape, lambda *_: (0,) * n)


def _rms(x, g):
    return x * lax.rsqrt(jnp.mean(x * x, axis=-1, keepdims=True) + EPS) * g


def _ln(y, g, b):
    mu = jnp.mean(y, axis=-1, keepdims=True)
    yc = y - mu
    var = jnp.mean(yc * yc, axis=-1, keepdims=True)
    return yc * lax.rsqrt(var + EPS) * g + b


def _dot(a, b):
    return jnp.dot(a, b, preferred_element_type=F32)


def _dot_nt(a, b):
    return lax.dot_general(a, b, (((1,), (1,)), ((), ())), preferred_element_type=F32)


def _dot_tn(a, b):
    return lax.dot_general(a, b, (((0,), (0,)), ((), ())), preferred_element_type=F32)


def _inproj_kernel(x_ref, g_ref, w0, w1, w2, w3, o0, o1, o2, o3):
    h = _rms(x_ref[...], g_ref[...]).astype(BF16)
    for w, o in ((w0, o0), (w1, o1), (w2, o2), (w3, o3)):
        o[...] = _dot(h, w[...])


def _inproj(x, g, ws, tm):
    m = x.shape[0]
    return pl.pallas_call(
        _inproj_kernel,
        grid=(m // tm,),
        in_specs=[pl.BlockSpec((tm, D_MODEL), lambda i: (i, 0)), _full((1, D_MODEL))]
        + [_full(w.shape) for w in ws],
        out_specs=[pl.BlockSpec((tm, w.shape[1]), lambda i: (i, 0)) for w in ws],
        out_shape=[jax.ShapeDtypeStruct((m, w.shape[1]), F32) for w in ws],
        compiler_params=_params(("parallel",)),
        name="inproj",
    )(x, g, *ws)


def _conv_post(acc, lg, lb):
    y = _ln(acc, lg, lb)
    return (y * jax.nn.sigmoid(y)).astype(BF16)


def _conv_kernel(z_ref, cw_ref, cb_ref, lg_ref, lb_ref, act_ref, st_ref, ph_sc, *, tl, sub):
    l = pl.program_id(1)

    @pl.when(l == 0)
    def _():
        ph_sc[0, 0:CONV_HALO, :] = jnp.zeros((CONV_HALO, CONV_DIM), F32)

    z = z_ref[...]
    ph_sc[0, CONV_HALO:CONV_HALO + tl, :] = z[:, :CONV_DIM] * jax.nn.sigmoid(z[:, CONV_DIM:])
    first = CONV_HALO - (CONV_WIDTH - 1)
    span = tl + CONV_HALO - SUBLANE
    for r in range(1, SUBLANE):
        for j0 in range(0, span, sub):
            n = min(sub, span - j0)
            ph_sc[r, j0:j0 + n, :] = ph_sc[0, j0 + r:j0 + r + n, :]
    for sb in range(tl // sub):
        acc = jnp.zeros((sub, CONV_DIM), F32) + cb_ref[...]
        for w in range(CONV_WIDTH):
            r, a = (first + w) % SUBLANE, (first + w) // SUBLANE
            acc = acc + ph_sc[r, sb * sub + SUBLANE * a:sb * sub + SUBLANE * a + sub, :] * cw_ref[w:w + 1, :]
        act_ref[sb * sub:(sb + 1) * sub, :] = _conv_post(acc, lg_ref[...], lb_ref[...])

    @pl.when(l == pl.num_programs(1) - 1)
    def _():
        st_ref[...] = ph_sc[0, tl + first:tl + CONV_HALO, :]

    ph_sc[0, 0:CONV_HALO, :] = ph_sc[0, tl:tl + CONV_HALO, :]


def _conv_prompt(z, cw, cb, lg, lb, nb, sl):
    tl = min(512, sl)
    sub = min(64, tl)
    nl = sl // tl
    vec = _full((1, CONV_DIM))
    return pl.pallas_call(
        functools.partial(_conv_kernel, tl=tl, sub=sub),
        grid=(nb, nl),
        in_specs=[pl.BlockSpec((tl, 2 * CONV_DIM), lambda b, l: (b * nl + l, 0)),
                  _full((CONV_WIDTH, CONV_DIM)), vec, vec, vec],
        out_specs=[pl.BlockSpec((tl, CONV_DIM), lambda b, l: (b * nl + l, 0)),
                   pl.BlockSpec((None, CONV_WIDTH - 1, CONV_DIM), lambda b, l: (b, 0, 0))],
        out_shape=[jax.ShapeDtypeStruct((nb * sl, CONV_DIM), BF16),
                   jax.ShapeDtypeStruct((nb, CONV_WIDTH - 1, CONV_DIM), F32)],
        scratch_shapes=[pltpu.VMEM((SUBLANE, tl + CONV_HALO, CONV_DIM), F32)],
        compiler_params=_params(("parallel", "arbitrary")),
        name="conv_prompt",
    )(z, cw, cb, lg, lb)


def _conv_step_kernel(buf_ref, z_ref, cw_ref, cb_ref, lg_ref, lb_ref, act_ref, glu_ref, *, ls):
    nbuf = CONV_WIDTH - 1
    glu = []
    for t in range(ls):
        z = z_ref[t]
        glu.append(z[:, :CONV_DIM] * jax.nn.sigmoid(z[:, CONV_DIM:]))
        glu_ref[t] = glu[t]
    for t in range(ls):
        acc = jnp.zeros(glu[0].shape, F32) + cb_ref[...]
        for w in range(CONV_WIDTH):
            j = t + w
            src = buf_ref[j] if j < nbuf else glu[j - nbuf]
            acc = acc + src * cw_ref[w:w + 1, :]
        act_ref[t] = _conv_post(acc, lg_ref[...], lb_ref[...])


def _conv_step(buf_t, z_t, cw, cb, lg, lb):
    ls, nseq, _ = z_t.shape
    nb = min(32, nseq)
    vec = _full((1, CONV_DIM))
    return pl.pallas_call(
        functools.partial(_conv_step_kernel, ls=ls),
        grid=(nseq // nb,),
        in_specs=[pl.BlockSpec((CONV_WIDTH - 1, nb, CONV_DIM), lambda i: (0, i, 0)),
                  pl.BlockSpec((ls, nb, 2 * CONV_DIM), lambda i: (0, i, 0)),
                  _full((CONV_WIDTH, CONV_DIM)), vec, vec, vec],
        out_specs=[pl.BlockSpec((ls, nb, CONV_DIM), lambda i: (0, i, 0)),
                   pl.BlockSpec((ls, nb, CONV_DIM), lambda i: (0, i, 0))],
        out_shape=[jax.ShapeDtypeStruct((ls, nseq, CONV_DIM), BF16),
                   jax.ShapeDtypeStruct((ls, nseq, CONV_DIM), F32)],
        compiler_params=_params(("parallel",)),
        name="conv_step",
    )(buf_t, z_t, cw, cb, lg, lb)


def _gla_levels(c, diag):
    levels, s = [], c // 2
    while s >= diag:
        levels.append(s)
        s //= 2
    return tuple(levels)


def _gla_sum_masks(c, levels):
    i = np.arange(c)[:, None]
    r = np.arange(c)[None, :]
    mats = [r <= i, r > i]
    for s in levels:
        ref = (i & ~(2 * s - 1)) + s - 1
        upper = (i & s) != 0
        mats.append(np.where(upper, (r > ref) & (r <= i), (r > i) & (r <= ref)))
    return np.concatenate(mats, axis=0).astype(np.float32)


def _gla_kernel(*refs, ns, **kw):
    z_ref, s0_ref, mall_ref, wg_ref, bg_ref, ng_ref, hsum_ref, act_ref, sout_ref, st_sc = refs
    ci = pl.program_id(1)

    @pl.when(ci == 0)
    def _():
        st_sc[...] = s0_ref[...]

    _gla_chunks(z_ref, act_ref, st_sc, mall_ref, wg_ref, bg_ref, ng_ref, hsum_ref, ns=ns, **kw)

    @pl.when(ci == pl.num_programs(1) - 1)
    def _():
        sout_ref[...] = st_sc[...]


def _gla_chunks(z_ref, act_ref, st_sc, mall_ref, wg_ref, bg_ref, ng_ref, hsum_ref, *, ns, c, diag, levels, valid, mmdt):
    seqs = range(ns)
    z = [z_ref[n] for n in seqs]
    q = [zz[:, 0:GLA_QK] * (GLA_DK ** -0.5) for zz in z]
    k = [zz[:, GLA_QK:2 * GLA_QK] for zz in z]
    v = [zz[:, 2 * GLA_QK:2 * GLA_QK + GLA_V] for zz in z]
    wg = wg_ref[...].astype(mmdt)
    x = [_dot(zz[:, 2 * GLA_QK + 2 * GLA_V:].astype(mmdt), wg) + bg_ref[...] for zz in z]
    g = [(jnp.minimum(xx, 0.0) - jnp.log(1.0 + jnp.exp(-jnp.abs(xx)))) * (1.0 / GLA_TAU) for xx in x]
    row = lax.broadcasted_iota(jnp.int32, (c, 1), 0)
    if valid < c:
        g = [jnp.where(row < valid, gg, 0.0) for gg in g]

    parts = []
    for gg in g:
        g1 = gg.astype(BF16)
        r1 = gg - g1.astype(F32)
        g2 = r1.astype(BF16)
        parts += [g1, g2, (r1 - g2.astype(F32)).astype(BF16)]
    e3 = _dot(mall_ref[...].astype(mmdt), jnp.concatenate(parts, axis=1).astype(mmdt))
    w = GLA_QK
    e = [e3[:, 3 * n * w:(3 * n + 1) * w] + e3[:, (3 * n + 1) * w:(3 * n + 2) * w] + e3[:, (3 * n + 2) * w:(3 * n + 3) * w]
         for n in seqs]
    b = [ee[0:c] for ee in e]
    qe = [(q[n] * jnp.exp(b[n])).astype(mmdt) for n in seqs]
    kd = [(k[n] * jnp.exp(e[n][c:2 * c])).astype(mmdt) for n in seqs]
    dlast = [jnp.exp(bb[c - 1:c, :]) for bb in b]
    vm = [vv.astype(mmdt) for vv in v]

    lvl = []
    if levels:
        ri = lax.broadcasted_iota(jnp.int32, (c, c), 0)
        rj = lax.broadcasted_iota(jnp.int32, (c, c), 1)
        for li, s in enumerate(levels):
            pair = ~(2 * s - 1)
            msk = ((ri & s) != 0) & ((rj & s) == 0) & ((ri & pair) == (rj & pair))
            dec = [jnp.exp(e[n][(2 + li) * c:(3 + li) * c]) for n in seqs]
            lvl.append(([(q[n] * dec[n]).astype(mmdt) for n in seqs], [(k[n] * dec[n]).astype(mmdt) for n in seqs], msk))

    outs = [[] for _ in seqs]
    for h in range(GLA_HEADS):
        ks = slice(h * GLA_DK, (h + 1) * GLA_DK)
        vs = slice(h * GLA_DV, (h + 1) * GLA_DV)
        st = [st_sc[n, h] for n in seqs]
        o_h = [_dot_nt(qe[n][:, ks], st[n].astype(mmdt)) for n in seqs]
        if lvl:
            a = [jnp.zeros((c, c), F32) for _ in seqs]
            for ql, kl, msk in lvl:
                a = [a[n] + jnp.where(msk, _dot_nt(ql[n][:, ks], kl[n][:, ks]), 0.0) for n in seqs]
            o_h = [o_h[n] + _dot(a[n].astype(mmdt), vm[n][:, vs]) for n in seqs]
        for n in seqs:
            st_sc[n, h] = st[n] * dlast[n][:, ks] + _dot_tn(vm[n][:, vs], kd[n][:, ks])
            outs[n].append(o_h[n])
    o = [jnp.concatenate(outs[n], axis=1) for n in seqs]

    hsum = hsum_ref[...].astype(mmdt)
    for d in range(min(diag, valid)):
        if d == 0:
            t = [q[n] * k[n] for n in seqs]
            vsh = v
        else:
            ok = (row & (diag - 1)) >= d
            dec = [jnp.exp(jnp.where(ok, b[n] - pltpu.roll(b[n], d, 0), 0.0)) for n in seqs]
            t = [jnp.where(ok, q[n] * pltpu.roll(k[n], d, 0) * dec[n], 0.0) for n in seqs]
            vsh = [pltpu.roll(v[n], d, 0) for n in seqs]
        o = [o[n] + _dot(t[n].astype(mmdt), hsum) * vsh[n] for n in seqs]

    ng = ng_ref[...]
    for n in seqs:
        normed = [_rms(o[n][:, h * GLA_DV:(h + 1) * GLA_DV], ng[:, h * GLA_DV:(h + 1) * GLA_DV])
                  for h in range(GLA_HEADS)]
        r = z[n][:, 2 * GLA_QK + GLA_V:2 * GLA_QK + 2 * GLA_V]
        act_ref[n] = (jnp.concatenate(normed, axis=1) * (r * jax.nn.sigmoid(r))).astype(act_ref.dtype)


def _gla(z, s0t, wg, bg, ng, nb, sl, c, valid, act_dtype, ns):
    assert nb % ns == 0 and sl % c == 0
    diag = min(GLA_DIAG, c)
    levels = _gla_levels(c, diag)
    mall = jnp.asarray(_gla_sum_masks(c, levels), BF16)
    hs = np.zeros((GLA_QK, GLA_V), np.float32)
    for h in range(GLA_HEADS):
        hs[h * GLA_DK:(h + 1) * GLA_DK, h * GLA_DV:(h + 1) * GLA_DV] = 1.0
    hsum = jnp.asarray(hs, BF16)
    nc = sl // c
    mmdt = BF16 if c % 16 == 0 else F32
    st_spec = pl.BlockSpec((ns, GLA_HEADS, GLA_DV, GLA_DK), lambda b, i: (b, 0, 0, 0))
    rows = lambda width: pl.BlockSpec((None, ns, c, width), lambda b, i: (b, 0, i, 0))
    act, st = pl.pallas_call(
        functools.partial(_gla_kernel, ns=ns, c=c, diag=diag, levels=levels, valid=valid, mmdt=mmdt),
        grid=(nb // ns, nc),
        in_specs=[rows(Z_GLA), st_spec, _full(mall.shape), _full(wg.shape), _full(bg.shape), _full(ng.shape),
                  _full(hsum.shape)],
        out_specs=[rows(GLA_V), st_spec],
        out_shape=[jax.ShapeDtypeStruct((nb // ns, ns, sl, GLA_V), act_dtype),
                   jax.ShapeDtypeStruct((nb, GLA_HEADS, GLA_DV, GLA_DK), F32)],
        scratch_shapes=[pltpu.VMEM((ns, GLA_HEADS, GLA_DV, GLA_DK), F32)],
        compiler_params=_params(("parallel", "arbitrary")),
        name="gla",
    )(z.reshape(nb // ns, ns, sl, Z_GLA), s0t, mall, wg, bg, ng, hsum)
    return act.reshape(nb * sl, GLA_V), st


def _gmlp_kernel(z_ref, ws_ref, bsb_ref, lg_ref, lb_ref, act_ref, *v_refs, nchunk):
    ti = lax.broadcasted_iota(jnp.int32, (GMLP_CHUNK, GMLP_CHUNK), 0)
    tj = lax.broadcasted_iota(jnp.int32, (GMLP_CHUNK, GMLP_CHUNK), 1)
    wt = [jnp.where(ti >= tj, ws_ref[g], 0.0).astype(BF16) for g in range(GMLP_GROUPS)]
    gc = GMLP_DIM // GMLP_GROUPS
    for ch in range(nchunk):
        rows = slice(ch * GMLP_CHUNK, (ch + 1) * GMLP_CHUNK)
        z = z_ref[rows, :]
        ge = z * (0.5 * (1.0 + jnp.tanh(math.sqrt(2.0 / math.pi) * (z + 0.044715 * (z * z * z)))))
        v = _ln(ge[:, GMLP_DIM:], lg_ref[...], lb_ref[...])
        if v_refs:
            v_refs[0][rows, :] = v
        vb = v.astype(BF16)
        s = jnp.concatenate([_dot(wt[g], vb[:, g * gc:(g + 1) * gc]) for g in range(GMLP_GROUPS)], axis=1)
        act_ref[rows, :] = (ge[:, :GMLP_DIM] * (s + bsb_ref[...])).astype(BF16)


def _gmlp(z, ws, bsb, lg, lb, emit_v):
    m = z.shape[0]
    tm = min(512, m)
    assert tm % GMLP_CHUNK == 0 and m % tm == 0
    blk = lambda n: pl.BlockSpec((tm, n), lambda i: (i, 0))
    vec = _full((1, GMLP_DIM))
    out_specs = [blk(GMLP_DIM)]
    out_shape = [jax.ShapeDtypeStruct((m, GMLP_DIM), BF16)]
    if emit_v:
        out_specs.append(blk(GMLP_DIM))
        out_shape.append(jax.ShapeDtypeStruct((m, GMLP_DIM), F32))
    return pl.pallas_call(
        functools.partial(_gmlp_kernel, nchunk=tm // GMLP_CHUNK),
        grid=(m // tm,),
        in_specs=[blk(2 * GMLP_DIM), _full(ws.shape), _full(bsb.shape), vec, vec],
        out_specs=out_specs,
        out_shape=out_shape,
        compiler_params=_params(("parallel",)),
        name="gmlp",
    )(z, ws, bsb, lg, lb)


def _mla_latents(z, ck, sk, gkv, lat_ref, kr_ref):
    c = _rms(z[:, MLA_Q_RANK:MLA_Q_RANK + MLA_KV_RANK], gkv)
    lat_ref[...] = c
    o = MLA_Q_RANK + MLA_KV_RANK
    kr = z[:, o:o + LANE] * ck + z[:, o + LANE:o + 2 * LANE] * sk
    kr_ref[...] = kr[:, :MLA_ROPE]
    return c, kr


def _mla_proj_kernel(z_ref, cq_ref, sq_ref, ck_ref, sk_ref, gq_ref, gkv_ref, wq1_ref, wq2_ref, wuk_ref,
                     wuv_ref, sel_ref, q_ref, k_ref, v_ref, lat_ref, kr_ref):
    z = z_ref[...]
    hq = _rms(z[:, :MLA_Q_RANK], gq_ref[...]).astype(BF16)
    q1 = _dot(hq, wq1_ref[...])
    q2 = _dot(hq, wq2_ref[...])
    cosq = cq_ref[...]
    sinq = sq_ref[...]
    for h in range(MLA_HEADS):
        hs = slice(h * LANE, (h + 1) * LANE)
        q_ref[:, hs] = ((q1[:, hs] * cosq + q2[:, hs] * sinq) * MLA_SCALE).astype(BF16)
    c, kr = _mla_latents(z, ck_ref[...], sk_ref[...], gkv_ref[...], lat_ref, kr_ref)
    cb = c.astype(BF16)
    k_ref[...] = (_dot(cb, wuk_ref[...]) + _dot(kr.astype(BF16), sel_ref[...])).astype(BF16)
    v_ref[...] = _dot(cb, wuv_ref[...]).astype(BF16)


def _mla_proj(z, tabs, gq, gkv, wq1, wq2, wuk, wuv, sel, sl):
    m = z.shape[0]
    tm = min(512, sl)
    nl = sl // tm
    row = lambda n: pl.BlockSpec((tm, n), lambda i: (i, 0))
    tab = pl.BlockSpec((tm, LANE), lambda i: (i % nl, 0))
    hd = MLA_HEADS * LANE
    return pl.pallas_call(
        _mla_proj_kernel,
        grid=(m // tm,),
        in_specs=[row(Z_MLA), tab, tab, tab, tab, _full(gq.shape), _full(gkv.shape), _full(wq1.shape),
                  _full(wq2.shape), _full(wuk.shape), _full(wuv.shape), _full(sel.shape)],
        out_specs=[row(hd), row(hd), row(MLA_HEADS * MLA_V), row(MLA_KV_RANK), row(MLA_ROPE)],
        out_shape=[jax.ShapeDtypeStruct((m, hd), BF16), jax.ShapeDtypeStruct((m, hd), BF16),
                   jax.ShapeDtypeStruct((m, MLA_HEADS * MLA_V), BF16),
                   jax.ShapeDtypeStruct((m, MLA_KV_RANK), F32), jax.ShapeDtypeStruct((m, MLA_ROPE), F32)],
        compiler_params=_params(("parallel",)),
        name="mla_proj",
    )(z, *tabs, gq, gkv, wq1, wq2, wuk, wuv, sel)


def _flash_kernel(q_ref, k_ref, v_ref, o_ref, m_sc, acc_sc, *, t, tk):
    qi = pl.program_id(2)
    m_sc[...] = jnp.full(m_sc.shape, -jnp.inf, F32)
    acc_sc[...] = jnp.zeros(acc_sc.shape, F32)
    ones = jnp.ones((tk, LANE), BF16)

    def block(j, diag_offset):
        start = pl.multiple_of(j * tk, tk)
        kblk = k_ref[pl.ds(start, tk), :]
        vext = jnp.concatenate([v_ref[pl.ds(start, tk), :], ones], axis=1)
        heads = range(2)
        s = [_dot_nt(q_ref[:, a * LANE:(a + 1) * LANE], kblk[:, a * LANE:(a + 1) * LANE]) for a in heads]
        if diag_offset is not None:
            ri = lax.broadcasted_iota(jnp.int32, (t, tk), 0)
            rj = lax.broadcasted_iota(jnp.int32, (t, tk), 1) + diag_offset
            s = [jnp.where(ri >= rj, ss, -jnp.inf) for ss in s]
        m_prev = [m_sc[a] for a in heads]
        m_new = [jnp.maximum(m_prev[a], jnp.max(s[a], axis=-1, keepdims=True)) for a in heads]
        alpha = [jnp.exp(m_prev[a] - m_new[a]) for a in heads]
        p = [jnp.exp(s[a] - jnp.concatenate([m_new[a]] * (tk // LANE), axis=1)).astype(BF16) for a in heads]
        pv = [_dot(p[a], vext) for a in heads]
        for a in heads:
            acc_sc[a] = jnp.concatenate([alpha[a]] * 2, axis=1) * acc_sc[a] + pv[a]
            m_sc[a] = m_new[a]

    per = t // tk
    nfull = qi * per

    def body(j, carry):
        block(j, None)
        return carry

    lax.fori_loop(0, nfull, body, 0)
    for d in range(per):
        block(nfull + d, d * tk)
    lane = lax.broadcasted_iota(jnp.int32, (t, LANE), 1)
    a0, a1 = acc_sc[0], acc_sc[1]
    o_ref[...] = jnp.where(lane < MLA_V, a0[:, :LANE] / a0[:, LANE:], a1[:, :LANE] / a1[:, LANE:]).astype(BF16)


def _flash(q, k, v, nb, sl):
    t = min(1024, sl)
    tk = min(512, t)
    nt = sl // t
    return pl.pallas_call(
        functools.partial(_flash_kernel, t=t, tk=tk),
        grid=(nb, MLA_HEADS // 2, nt),
        in_specs=[pl.BlockSpec((t, 2 * LANE), lambda b, h, i: (b * nt + i, h)),
                  pl.BlockSpec((sl, 2 * LANE), lambda b, h, i: (b, h)),
                  pl.BlockSpec((sl, LANE), lambda b, h, i: (b, h))],
        out_specs=pl.BlockSpec((t, LANE), lambda b, h, i: (b * nt + i, h)),
        out_shape=jax.ShapeDtypeStruct((nb * sl, MLA_HEADS * MLA_V), BF16),
        scratch_shapes=[pltpu.VMEM((2, t, LANE), F32), pltpu.VMEM((2, t, 2 * LANE), F32)],
        compiler_params=_params(("parallel", "parallel", "arbitrary")),
        name="mla_flash",
    )(q, k, v)


def _mla_proj_step_kernel(z_ref, c8_ref, s8_ref, ck_ref, sk_ref, gq_ref, gkv_ref, wqn_ref, wqr_ref, wqs_ref,
                          wukt_ref, qlat_ref, qr_ref, lat_ref, kr_ref):
    z = z_ref[...]
    hq = _rms(z[:, :MLA_Q_RANK], gq_ref[...]).astype(BF16)
    qn = _dot(hq, wqn_ref[...]).astype(BF16)
    for h in range(MLA_HEADS):
        lat = _dot(qn[:, h * LANE:(h + 1) * LANE], wukt_ref[h])
        qlat_ref[:, h * MLA_KV_RANK:(h + 1) * MLA_KV_RANK] = (lat * MLA_SCALE).astype(BF16)
    qr = _dot(hq, wqr_ref[...]) * c8_ref[...] + _dot(hq, wqs_ref[...]) * s8_ref[...]
    qr_ref[...] = (qr * MLA_SCALE).astype(BF16)
    _mla_latents(z, ck_ref[...], sk_ref[...], gkv_ref[...], lat_ref, kr_ref)


def _mla_proj_step(z, tabs, gq, gkv, wqn, wqr, wqs, wukt):
    m = z.shape[0]
    args = (z, *tabs, gq, gkv, wqn, wqr, wqs, wukt)
    return pl.pallas_call(
        _mla_proj_step_kernel,
        grid=(1,),
        in_specs=[_full(a.shape) for a in args],
        out_specs=[_full((m, MLA_HEADS * MLA_KV_RANK)), _full((m, MLA_HEADS * MLA_ROPE)),
                   _full((m, MLA_KV_RANK)), _full((m, MLA_ROPE))],
        out_shape=[jax.ShapeDtypeStruct((m, MLA_HEADS * MLA_KV_RANK), BF16),
                   jax.ShapeDtypeStruct((m, MLA_HEADS * MLA_ROPE), BF16),
                   jax.ShapeDtypeStruct((m, MLA_KV_RANK), F32), jax.ShapeDtypeStruct((m, MLA_ROPE), F32)],
        compiler_params=_params(("arbitrary",)),
        name="mla_proj_step",
    )(*args)


def _paged_kernel(pt_ref, ql_ref, qr_ref, cn_ref, rn_ref, lat_hbm, krt_hbm, o_ref,
                  lat_buf, kr_buf, sem, m_sc, l_sc, acc_sc, kb_sc, rb_sc, *, layer, pp, ls, npad):
    j = pl.program_id(1)
    nj = pl.num_programs(1)
    step = pl.program_id(0) * nj + j
    nsteps = pl.num_programs(0) * nj
    slot = lax.rem(step, 2)

    def page_copies(at_step, sl):
        out = []
        for i in range(pp):
            pg = pt_ref[at_step * pp + i]
            out.append(pltpu.make_async_copy(lat_hbm.at[layer, pg], lat_buf.at[sl, i], sem.at[0, sl]))
            out.append(pltpu.make_async_copy(krt_hbm.at[layer, pg], kr_buf.at[sl, i], sem.at[1, sl]))
        return out

    @pl.when(step == 0)
    def _():
        for cp in page_copies(step, slot):
            cp.start()

    @pl.when(step + 1 < nsteps)
    def _():
        for cp in page_copies(step + 1, 1 - slot):
            cp.start()

    ql = ql_ref[...]
    qr = qr_ref[...]
    nrow = ql.shape[0]

    @pl.when(j == 0)
    def _():
        cn = cn_ref[...]
        s = _dot_nt(ql.astype(F32), cn) + _dot_nt(qr.astype(F32), rn_ref[...])
        tok = jnp.right_shift(lax.broadcasted_iota(jnp.int32, (nrow, npad), 0), int(math.log2(MLA_HEADS)))
        key = lax.broadcasted_iota(jnp.int32, (nrow, npad), 1)
        s = jnp.where((key <= tok) & (key < ls), s, -jnp.inf)
        m = jnp.max(s, axis=-1, keepdims=True)
        p = jnp.exp(s - m)
        m_sc[...] = m
        l_sc[...] = jnp.sum(p, axis=-1, keepdims=True)
        acc_sc[...] = _dot(p, cn)

    for cp in page_copies(step, slot):
        cp.wait()
    for i in range(pp):
        kb_sc[i * LANE:(i + 1) * LANE, :] = lat_buf[slot, i].astype(BF16)
        rb_sc[:, i * LANE:(i + 1) * LANE] = kr_buf[slot, i].astype(BF16)
    kb = kb_sc[...]
    s = _dot_nt(ql, kb) + _dot(qr, rb_sc[...])
    m_prev = m_sc[...]
    m_new = jnp.maximum(m_prev, jnp.max(s, axis=-1, keepdims=True))
    alpha = jnp.exp(m_prev - m_new)
    p = jnp.exp(s - m_new)
    l_sc[...] = alpha * l_sc[...] + jnp.sum(p, axis=-1, keepdims=True)
    acc_sc[...] = alpha * acc_sc[...] + _dot(p.astype(BF16), kb)
    m_sc[...] = m_new

    @pl.when(j == nj - 1)
    def _():
        o_ref[...] = (acc_sc[...] / l_sc[...]).astype(BF16)


def _paged_attention(pt_flat, ql, qr, cn, rn, cache_lat, cache_krt, layer, n_pages, ls):
    nseq, nrow, _ = ql.shape
    page = cache_lat.shape[2]
    assert page == LANE
    pp = min(32, n_pages)
    npad = cn.shape[1]
    seq = lambda shp: pl.BlockSpec((None,) + shp, lambda b, j, pt: (b, 0, 0))
    hbm = pl.BlockSpec(memory_space=pl.ANY)
    grid_spec = pltpu.PrefetchScalarGridSpec(
        num_scalar_prefetch=1,
        grid=(nseq, n_pages // pp),
        in_specs=[seq((nrow, MLA_KV_RANK)), seq((nrow, MLA_ROPE)), seq((npad, MLA_KV_RANK)),
                  seq((npad, MLA_ROPE)), hbm, hbm],
        out_specs=seq((nrow, MLA_KV_RANK)),
        scratch_shapes=[pltpu.VMEM((2, pp, page, MLA_KV_RANK), F32), pltpu.VMEM((2, pp, MLA_ROPE, page), F32),
                        pltpu.SemaphoreType.DMA((2, 2)),
                        pltpu.VMEM((nrow, 1), F32), pltpu.VMEM((nrow, 1), F32),
                        pltpu.VMEM((nrow, MLA_KV_RANK), F32),
                        pltpu.VMEM((pp * page, MLA_KV_RANK), BF16), pltpu.VMEM((MLA_ROPE, pp * page), BF16)],
    )
    return pl.pallas_call(
        functools.partial(_paged_kernel, layer=layer, pp=pp, ls=ls, npad=npad),
        grid_spec=grid_spec,
        out_shape=jax.ShapeDtypeStruct((nseq, nrow, MLA_KV_RANK), BF16),
        compiler_params=_params(("arbitrary", "arbitrary")),
        name="mla_paged",
    )(pt_flat, ql, qr, cn, rn, cache_lat, cache_krt)


def _mla_up_kernel(o_ref, w_ref, out_ref):
    acc = _dot(o_ref[:, 0:MLA_KV_RANK], w_ref[0])
    for h in range(1, MLA_HEADS):
        acc = acc + _dot(o_ref[:, h * MLA_KV_RANK:(h + 1) * MLA_KV_RANK], w_ref[h])
    out_ref[...] = acc.astype(BF16)


def _mla_up(o_lat, wuvp):
    m = o_lat.shape[0]
    return pl.pallas_call(
        _mla_up_kernel,
        grid=(1,),
        in_specs=[_full(o_lat.shape), _full(wuvp.shape)],
        out_specs=_full((m, MLA_HEADS * MLA_V)),
        out_shape=jax.ShapeDtypeStruct((m, MLA_HEADS * MLA_V), BF16),
        compiler_params=_params(("arbitrary",)),
        name="mla_up",
    )(o_lat, wuvp)


def _merge_kernel(x_ref, a_ref, b_ref, c_ref, d_ref, g_ref, wg_ref, wa_ref, wb_ref, wc_ref, wd_ref, wo_ref,
                  gp_ref, o_ref):
    x = x_ref[...]
    h = _rms(x, g_ref[...]).astype(BF16)
    merged = None
    for i, (act, w) in enumerate(((a_ref, wa_ref), (b_ref, wb_ref), (c_ref, wc_ref), (d_ref, wd_ref))):
        gate = jax.nn.sigmoid(_dot(h, wg_ref[:, i * D_MODEL:(i + 1) * D_MODEL]))
        term = gate * _dot(act[...], w[...])
        merged = term if merged is None else merged + term
    y = _dot(merged.astype(BF16), wo_ref[...])
    o_ref[...] = x + _rms(y, gp_ref[...])


def _merge(x, acts, g, wgate, wouts, wo, gp):
    m = x.shape[0]
    tm = min(256, m)
    row = lambda n: pl.BlockSpec((tm, n), lambda i: (i, 0))
    vec = _full((1, D_MODEL))
    return pl.pallas_call(
        _merge_kernel,
        grid=(m // tm,),
        in_specs=[row(D_MODEL)] + [row(a.shape[1]) for a in acts] + [vec, _full(wgate.shape)]
        + [_full(w.shape) for w in wouts] + [_full(wo.shape), vec],
        out_specs=row(D_MODEL),
        out_shape=jax.ShapeDtypeStruct((m, D_MODEL), F32),
        compiler_params=_params(("parallel",)),
        name="merge",
    )(x, *acts, g, wgate, *wouts, wo, gp)


def _ffn_kernel(x_ref, g_ref, w1_ref, w2_ref, gp_ref, o_ref, h_sc, acc_sc):
    j = pl.program_id(1)

    @pl.when(j == 0)
    def _():
        h_sc[...] = _rms(x_ref[...], g_ref[...]).astype(BF16)
        acc_sc[...] = jnp.zeros(acc_sc.shape, F32)

    t = jnp.maximum(_dot(h_sc[...], w1_ref[...]), 0.0)
    acc_sc[...] += _dot((t * t).astype(BF16), w2_ref[...])

    @pl.when(j == pl.num_programs(1) - 1)
    def _():
        o_ref[...] = x_ref[...] + _rms(acc_sc[...], gp_ref[...])


def _ffn(x, g, w1, w2, gp):
    m = x.shape[0]
    tm = min(1024, m)
    tf = 1024
    vec = _full((1, D_MODEL))
    return pl.pallas_call(
        _ffn_kernel,
        grid=(m // tm, D_FF // tf),
        in_specs=[pl.BlockSpec((tm, D_MODEL), lambda i, j: (i, 0)), vec,
                  pl.BlockSpec((D_MODEL, tf), lambda i, j: (0, j)),
                  pl.BlockSpec((tf, D_MODEL), lambda i, j: (j, 0)), vec],
        out_specs=pl.BlockSpec((tm, D_MODEL), lambda i, j: (i, 0)),
        out_shape=jax.ShapeDtypeStruct((m, D_MODEL), F32),
        scratch_shapes=[pltpu.VMEM((tm, D_MODEL), BF16), pltpu.VMEM((tm, D_MODEL), F32)],
        compiler_params=_params(("parallel", "arbitrary")),
        name="ffn",
    )(x, g, w1, w2, gp)


def _swap_halves(w):
    half = w.shape[-1] // 2
    return jnp.concatenate([w[..., half:], w[..., :half]], axis=-1)


def _prep_layer(p):
    w_in = p['w_in']
    o_gla = 2 * CONV_DIM
    o_gd = o_gla + 2 * GLA_QK + GLA_V
    o_r = o_gd + GLA_GATE_RANK
    o_gmlp = o_r + GLA_V
    o_mla = o_gmlp + 2 * GMLP_DIM
    o_kr = o_mla + MLA_Q_RANK + MLA_KV_RANK
    o_gate = o_kr + MLA_ROPE
    rows = w_in.shape[0]
    zpad = lambda n: jnp.zeros((rows, n), w_in.dtype)
    w_kr = w_in[:, o_kr:o_gate]
    q = {}
    q['w_conv'] = w_in[:, :o_gla].astype(BF16)
    q['w_gla'] = jnp.concatenate([w_in[:, o_gla:o_gd], w_in[:, o_r:o_gmlp], w_in[:, o_gd:o_r],
                                  zpad(LANE - GLA_GATE_RANK)], axis=1).astype(BF16)
    q['w_gmlp'] = w_in[:, o_gmlp:o_mla].astype(BF16)
    q['w_mla'] = jnp.concatenate([w_in[:, o_mla:o_kr], w_kr, zpad(LANE - MLA_ROPE), _swap_halves(w_kr),
                                  zpad(LANE - MLA_ROPE)], axis=1).astype(BF16)
    q['w_gate'] = w_in[:, o_gate:].astype(BF16)

    q['gla_wg'] = jnp.concatenate([p['gla_gate_up'], jnp.zeros((LANE - GLA_GATE_RANK, GLA_QK), F32)],
                                  axis=0).astype(BF16)
    q['gla_bg'] = p['gla_gate_b'].reshape(1, GLA_QK)
    q['gla_ng'] = p['gla_norm_g'].reshape(1, GLA_V)

    rq = MLA_Q_RANK
    wq = p['mla_w_uq'].reshape(rq, MLA_HEADS, MLA_NOPE + MLA_ROPE)
    nope, rope = wq[..., :MLA_NOPE], wq[..., MLA_NOPE:]
    rope_sw = _swap_halves(rope)
    zq = lambda n: jnp.zeros((rq, MLA_HEADS, n), F32)
    hd = MLA_HEADS * LANE
    q['wq1'] = jnp.concatenate([nope, rope, zq(LANE - MLA_NOPE - MLA_ROPE)], -1).reshape(rq, hd).astype(BF16)
    q['wq2'] = jnp.concatenate([zq(MLA_NOPE), rope_sw, zq(LANE - MLA_NOPE - MLA_ROPE)], -1).reshape(rq, hd).astype(BF16)
    q['wqn'] = jnp.concatenate([nope, zq(LANE - MLA_NOPE)], -1).reshape(rq, hd).astype(BF16)
    q['wqr'] = rope.reshape(rq, MLA_HEADS * MLA_ROPE).astype(BF16)
    q['wqs'] = rope_sw.reshape(rq, MLA_HEADS * MLA_ROPE).astype(BF16)
    wuk, wuv = p['mla_w_uk'], p['mla_w_uv']
    zk = jnp.zeros((MLA_KV_RANK, MLA_HEADS, LANE - MLA_NOPE), F32)
    q['wuk'] = jnp.concatenate([wuk, zk], -1).reshape(MLA_KV_RANK, hd).astype(BF16)
    q['wuv'] = wuv.reshape(MLA_KV_RANK, MLA_HEADS * MLA_V).astype(BF16)
    q['wukt'] = jnp.concatenate([wuk, zk], -1).transpose(1, 2, 0).astype(BF16)
    eye = jnp.eye(MLA_HEADS, dtype=F32)
    q['wuvp'] = (wuv.transpose(1, 0, 2)[:, :, None, :] * eye[:, None, :, None]).reshape(
        MLA_HEADS, MLA_KV_RANK, MLA_HEADS * MLA_V).astype(BF16)

    for name in ('conv_out', 'gla_out', 'gmlp_out', 'mla_out', 'w_o', 'ffn_w1', 'ffn_w2'):
        q[name] = p[name].astype(BF16)
    for name in ('g_pre_mix', 'g_post_mix', 'g_pre_ffn', 'g_post_ffn', 'conv_b', 'conv_ln_g', 'conv_ln_b',
                 'gmlp_ln_g', 'gmlp_ln_b', 'mla_q_norm_g', 'mla_kv_norm_g'):
        q[name] = p[name].reshape(1, -1)
    q['conv_w'] = p['conv_w']
    q['gmlp_ws'] = p['gmlp_ws']
    q['gmlp_bs'] = p['gmlp_bs']
    return q


def _rope_tables(pos):
    half = MLA_ROPE // 2
    inv = ROPE_THETA ** (-jnp.arange(half, dtype=F32) / half)
    ang = pos.astype(F32)[:, None] * inv[None, :]
    cos, sin = jnp.cos(ang), jnp.sin(ang)
    return jnp.concatenate([cos, cos], -1), jnp.concatenate([-sin, sin], -1)


def _key_sel():
    sel = np.zeros((LANE, MLA_HEADS * LANE), np.float32)
    for h in range(MLA_HEADS):
        for j in range(MLA_ROPE):
            sel[j, h * LANE + MLA_NOPE + j] = 1.0
    return jnp.asarray(sel, BF16)


def _pad_lanes(t, left, total):
    n = t.shape[0]
    return jnp.concatenate([jnp.zeros((n, left), F32), t, jnp.zeros((n, total - left - t.shape[1]), F32)], -1)


def _mix_tail(x, acts, q):
    x = _merge(x, acts, q['g_pre_mix'], q['w_gate'],
               (q['conv_out'], q['gla_out'], q['gmlp_out'], q['mla_out']), q['w_o'], q['g_post_mix'])
    return _ffn(x, q['g_pre_ffn'], q['ffn_w1'], q['ffn_w2'], q['g_post_ffn'])


def _layer_prompt(x, q, tabs, sel, nb, sl):
    m = nb * sl
    z_conv, z_gla, z_gmlp, z_mla = _inproj(x, q['g_pre_mix'], (q['w_conv'], q['w_gla'], q['w_gmlp'], q['w_mla']),
                                           min(256, m))
    act_a, conv_st = _conv_prompt(z_conv, q['conv_w'], q['conv_b'], q['conv_ln_g'], q['conv_ln_b'], nb, sl)
    c = min(GLA_CHUNK, sl)
    s0t = jnp.zeros((nb, GLA_HEADS, GLA_DV, GLA_DK), F32)
    act_b, st = _gla(z_gla, s0t, q['gla_wg'], q['gla_bg'], q['gla_ng'], nb, sl, c, c, BF16, GLA_SEQS_PROMPT if nb % GLA_SEQS_PROMPT == 0 else 1)
    bsb = jnp.repeat(q['gmlp_bs'].T, GMLP_DIM // GMLP_GROUPS, axis=1)
    (act_c,) = _gmlp(z_gmlp, q['gmlp_ws'], bsb, q['gmlp_ln_g'], q['gmlp_ln_b'], False)
    qp, kp, v, lat, kr = _mla_proj(z_mla, tabs, q['mla_q_norm_g'], q['mla_kv_norm_g'], q['wq1'], q['wq2'],
                                   q['wuk'], q['wuv'], sel, sl)
    act_d = _flash(qp, kp, v, nb, sl)
    x = _mix_tail(x, (act_a, act_b, act_c, act_d), q)
    return x, lat, kr, conv_st, st.swapaxes(-1, -2)


def _layer_sample(x, q, tabs, nseq, ls, conv_buf, gla_s, cache_lat, cache_kr, pt_flat, layer, n_pages):
    m = nseq * ls
    z_conv, z_gla, z_gmlp, z_mla = _inproj(x, q['g_pre_mix'], (q['w_conv'], q['w_gla'], q['w_gmlp'], q['w_mla']),
                                           min(256, m))
    act_t, glu_t = _conv_step(conv_buf.transpose(1, 0, 2), z_conv.reshape(nseq, ls, -1).transpose(1, 0, 2),
                              q['conv_w'], q['conv_b'], q['conv_ln_g'], q['conv_ln_b'])
    act_a = act_t.transpose(1, 0, 2).reshape(m, CONV_DIM)
    conv_st = jnp.concatenate([conv_buf, glu_t.transpose(1, 0, 2)], axis=1)[:, -(CONV_WIDTH - 1):]

    lp = 8
    pad = lambda z: jnp.pad(z.reshape(nseq, ls, -1), ((0, 0), (0, lp - ls), (0, 0))).reshape(nseq * lp, -1)
    unpad = lambda a: a.reshape(nseq, lp, -1)[:, :ls].reshape(m, -1)
    act_b, st = _gla(pad(z_gla), gla_s.swapaxes(-1, -2), q['gla_wg'], q['gla_bg'], q['gla_ng'], nseq, lp, lp, ls, F32,
                     GLA_SEQS_STEP if nseq % GLA_SEQS_STEP == 0 else 1)
    act_b = unpad(act_b).astype(BF16)

    per = GMLP_CHUNK // lp
    eye = jnp.eye(per, dtype=F32)
    wk = (eye[None, :, None, :, None] * q['gmlp_ws'][:, None, :lp, None, :lp]).reshape(
        GMLP_GROUPS, GMLP_CHUNK, GMLP_CHUNK)
    bsb = jnp.repeat(jnp.tile(q['gmlp_bs'][:, :lp], (1, per)).T, GMLP_DIM // GMLP_GROUPS, axis=1)
    act_c, v_rows = _gmlp(pad(z_gmlp), wk, bsb, q['gmlp_ln_g'], q['gmlp_ln_b'], True)
    act_c, v_rows = unpad(act_c), unpad(v_rows)

    ql, qr, lat, kr = _mla_proj_step(z_mla, tabs, q['mla_q_norm_g'], q['mla_kv_norm_g'], q['wqn'], q['wqr'],
                                     q['wqs'], q['wukt'])
    npad = 8
    padk = lambda t: jnp.pad(t.reshape(nseq, ls, -1), ((0, 0), (0, npad - ls), (0, 0)))
    o_lat = _paged_attention(pt_flat, ql.reshape(nseq, ls * MLA_HEADS, MLA_KV_RANK),
                             qr.reshape(nseq, ls * MLA_HEADS, MLA_ROPE), padk(lat), padk(kr),
                             cache_lat, cache_kr, layer, n_pages, ls)
    act_d = _mla_up(o_lat.reshape(m, MLA_HEADS * MLA_KV_RANK), q['wuvp'])
    x = _mix_tail(x, (act_a, act_b, act_c, act_d), q)
    return x, lat, kr, conv_st, st.swapaxes(-1, -2), v_rows


def kernel(x_prompt, x_sample, cache_mla_latent, cache_mla_krope, state_conv, state_gla, page_table,
           g_pre_mix, w_in, conv_w, conv_b, conv_ln_g, conv_ln_b, conv_out,
           gla_gate_up, gla_gate_b, gla_norm_g, gla_out,
           gmlp_ln_g, gmlp_ln_b, gmlp_ws, gmlp_bs, gmlp_out,
           mla_q_norm_g, mla_w_uq, mla_kv_norm_g, mla_w_uk, mla_w_uv, mla_out,
           w_o, g_post_mix, g_pre_ffn, ffn_w1, ffn_w2, g_post_ffn):
    params = {'g_pre_mix': g_pre_mix, 'w_in': w_in, 'conv_w': conv_w, 'conv_b': conv_b, 'conv_ln_g': conv_ln_g,
              'conv_ln_b': conv_ln_b, 'conv_out': conv_out, 'gla_gate_up': gla_gate_up, 'gla_gate_b': gla_gate_b,
              'gla_norm_g': gla_norm_g, 'gla_out': gla_out, 'gmlp_ln_g': gmlp_ln_g, 'gmlp_ln_b': gmlp_ln_b,
              'gmlp_ws': gmlp_ws, 'gmlp_bs': gmlp_bs, 'gmlp_out': gmlp_out, 'mla_q_norm_g': mla_q_norm_g,
              'mla_w_uq': mla_w_uq, 'mla_kv_norm_g': mla_kv_norm_g, 'mla_w_uk': mla_w_uk, 'mla_w_uv': mla_w_uv,
              'mla_out': mla_out, 'w_o': w_o, 'g_post_mix': g_post_mix, 'g_pre_ffn': g_pre_ffn,
              'ffn_w1': ffn_w1, 'ffn_w2': ffn_w2, 'g_post_ffn': g_post_ffn}
    depth = w_in.shape[0]
    nb, sl, _ = x_prompt.shape
    nseq, ls, _ = x_sample.shape
    n_pages = page_table.shape[1]
    past_len = n_pages * cache_mla_latent.shape[2]

    cos_p, sin_p = _rope_tables(jnp.arange(sl, dtype=jnp.int32))
    tabs_p = (jnp.concatenate([jnp.ones((sl, MLA_NOPE), F32), cos_p, jnp.zeros((sl, LANE - MLA_NOPE - MLA_ROPE), F32)], -1),
              _pad_lanes(sin_p, MLA_NOPE, LANE), _pad_lanes(cos_p, 0, LANE), _pad_lanes(sin_p, 0, LANE))
    cos_s, sin_s = _rope_tables(past_len + jnp.arange(ls, dtype=jnp.int32))
    cos_s, sin_s = jnp.tile(cos_s, (nseq, 1)), jnp.tile(sin_s, (nseq, 1))
    tabs_s = (jnp.tile(cos_s, (1, MLA_HEADS)), jnp.tile(sin_s, (1, MLA_HEADS)),
              _pad_lanes(cos_s, 0, LANE), _pad_lanes(sin_s, 0, LANE))
    sel = _key_sel()
    pt_flat = page_table.reshape(-1)
    cache_krt = cache_mla_krope.swapaxes(-1, -2)

    xp = x_prompt.reshape(nb * sl, D_MODEL)
    xs = x_sample.reshape(nseq * ls, D_MODEL)
    outs_p, outs_s = [], []
    for i in range(depth):
        q = _prep_layer({name: w[i] for name, w in params.items()})
        xp, lat, kr, cst, st = _layer_prompt(xp, q, tabs_p, sel, nb, sl)
        outs_p.append((lat.reshape(nb, sl, -1), kr.reshape(nb, sl, -1), cst, st))
        xs, lat, kr, cst, st, vr = _layer_sample(xs, q, tabs_s, nseq, ls, state_conv[i], state_gla[i],
                                                 cache_mla_latent, cache_krt, pt_flat, i, n_pages)
        outs_s.append((lat.reshape(nseq, ls, -1), kr.reshape(nseq, ls, -1), cst, st, vr.reshape(nseq, ls, -1)))
    stack = lambda outs, j: jnp.stack([o[j] for o in outs])
    return (xp.reshape(nb, sl, D_MODEL), xs.reshape(nseq, ls, D_MODEL),
            stack(outs_p, 0), stack(outs_p, 1), stack(outs_p, 2), stack(outs_p, 3),
            stack(outs_s, 0), stack(outs_s, 1), stack(outs_s, 2), stack(outs_s, 3), stack(outs_s, 4))
```

```python
import functools
import math

import numpy as np
import jax
import jax.numpy as jnp
from jax import lax
from jax.experimental import pallas as pl
from jax.experimental.pallas import tpu as pltpu

F32 = jnp.float32
BF16 = jnp.bfloat16
EPS = 1e-6

D_MODEL = 1024
CONV_DIM = 512
CONV_WIDTH = 31
CONV_HALO = 32
GLA_HEADS = 4
GLA_DK = 64
GLA_DV = 128
GLA_GATE_RANK = 16
GLA_TAU = 16.0
GLA_CHUNK = 64
GLA_DIAG = 8
GLA_SEQS_PROMPT = 4
GLA_SEQS_STEP = 8
GMLP_DIM = 512
GMLP_GROUPS = 4
GMLP_CHUNK = 128
MLA_HEADS = 8
MLA_Q_RANK = 384
MLA_KV_RANK = 256
MLA_NOPE = 64
MLA_ROPE = 32
MLA_V = 64
MLA_SCALE = (MLA_NOPE + MLA_ROPE) ** -0.5
ROPE_THETA = 10000.0
D_FF = 4 * D_MODEL
LANE = 128
SUBLANE = 8
VMEM_LIMIT = 56 * 1024 * 1024

GLA_QK = GLA_HEADS * GLA_DK
GLA_V = GLA_HEADS * GLA_DV
Z_GLA = 2 * GLA_QK + 2 * GLA_V + LANE
Z_MLA = MLA_Q_RANK + MLA_KV_RANK + 2 * LANE


def _params(sem):
    return pltpu.CompilerParams(dimension_semantics=sem, vmem_limit_bytes=VMEM_LIMIT)


def _full(shape):
    n = len(shape)
    return pl.BlockSpec(shape, lambda *_: (0,) * n)


def _rms(x, g):
    return x * lax.rsqrt(jnp.mean(x * x, axis=-1, keepdims=True) + EPS) * g


def _ln(y, g, b):
    mu = jnp.mean(y, axis=-1, keepdims=True)
    yc = y - mu
    var = jnp.mean(yc * yc, axis=-1, keepdims=True)
    return yc * lax.rsqrt(var + EPS) * g + b


def _dot(a, b):
    return jnp.dot(a, b, preferred_element_type=F32)


def _dot_nt(a, b):
    return lax.dot_general(a, b, (((1,), (1,)), ((), ())), preferred_element_type=F32)


def _dot_tn(a, b):
    return lax.dot_general(a, b, (((0,), (0,)), ((), ())), preferred_element_type=F32)


def _inproj_kernel(x_ref, g_ref, w0, w1, w2, w3, o0, o1, o2, o3):
    h = _rms(x_ref[...], g_ref[...]).astype(BF16)
    for w, o in ((w0, o0), (w1, o1), (w2, o2), (w3, o3)):
        o[...] = _dot(h, w[...])


def _inproj(x, g, ws, tm):
    m = x.shape[0]
    return pl.pallas_call(
        _inproj_kernel,
        grid=(m // tm,),
        in_specs=[pl.BlockSpec((tm, D_MODEL), lambda i: (i, 0)), _full((1, D_MODEL))]
        + [_full(w.shape) for w in ws],
        out_specs=[pl.BlockSpec((tm, w.shape[1]), lambda i: (i, 0)) for w in ws],
        out_shape=[jax.ShapeDtypeStruct((m, w.shape[1]), F32) for w in ws],
        compiler_params=_params(("parallel",)),
        name="inproj",
    )(x, g, *ws)


def _conv_post(acc, lg, lb):
    y = _ln(acc, lg, lb)
    return (y * jax.nn.sigmoid(y)).astype(BF16)


def _conv_kernel(z_ref, cw_ref, cb_ref, lg_ref, lb_ref, act_ref, st_ref, ph_sc, *, tl, sub):
    l = pl.program_id(1)

    @pl.when(l == 0)
    def _():
        ph_sc[0, 0:CONV_HALO, :] = jnp.zeros((CONV_HALO, CONV_DIM), F32)

    z = z_ref[...]
    ph_sc[0, CONV_HALO:CONV_HALO + tl, :] = z[:, :CONV_DIM] * jax.nn.sigmoid(z[:, CONV_DIM:])
    first = CONV_HALO - (CONV_WIDTH - 1)
    span = tl + CONV_HALO - SUBLANE
    for r in range(1, SUBLANE):
        for j0 in range(0, span, sub):
            n = min(sub, span - j0)
            ph_sc[r, j0:j0 + n, :] = ph_sc[0, j0 + r:j0 + r + n, :]
    for sb in range(tl // sub):
        acc = jnp.zeros((sub, CONV_DIM), F32) + cb_ref[...]
        for w in range(CONV_WIDTH):
            r, a = (first + w) % SUBLANE, (first + w) // SUBLANE
            acc = acc + ph_sc[r, sb * sub + SUBLANE * a:sb * sub + SUBLANE * a + sub, :] * cw_ref[w:w + 1, :]
        act_ref[sb * sub:(sb + 1) * sub, :] = _conv_post(acc, lg_ref[...], lb_ref[...])

    @pl.when(l == pl.num_programs(1) - 1)
    def _():
        st_ref[...] = ph_sc[0, tl + first:tl + CONV_HALO, :]

    ph_sc[0, 0:CONV_HALO, :] = ph_sc[0, tl:tl + CONV_HALO, :]


def _conv_prompt(z, cw, cb, lg, lb, nb, sl):
    tl = min(512, sl)
    sub = min(64, tl)
    nl = sl // tl
    vec = _full((1, CONV_DIM))
    return pl.pallas_call(
        functools.partial(_conv_kernel, tl=tl, sub=sub),
        grid=(nb, nl),
        in_specs=[pl.BlockSpec((tl, 2 * CONV_DIM), lambda b, l: (b * nl + l, 0)),
                  _full((CONV_WIDTH, CONV_DIM)), vec, vec, vec],
        out_specs=[pl.BlockSpec((tl, CONV_DIM), lambda b, l: (b * nl + l, 0)),
                   pl.BlockSpec((None, CONV_WIDTH - 1, CONV_DIM), lambda b, l: (b, 0, 0))],
        out_shape=[jax.ShapeDtypeStruct((nb * sl, CONV_DIM), BF16),
                   jax.ShapeDtypeStruct((nb, CONV_WIDTH - 1, CONV_DIM), F32)],
        scratch_shapes=[pltpu.VMEM((SUBLANE, tl + CONV_HALO, CONV_DIM), F32)],
        compiler_params=_params(("parallel", "arbitrary")),
        name="conv_prompt",
    )(z, cw, cb, lg, lb)


def _conv_step_kernel(buf_ref, z_ref, cw_ref, cb_ref, lg_ref, lb_ref, act_ref, glu_ref, *, ls):
    nbuf = CONV_WIDTH - 1
    glu = []
    for t in range(ls):
        z = z_ref[t]
        glu.append(z[:, :CONV_DIM] * jax.nn.sigmoid(z[:, CONV_DIM:]))
        glu_ref[t] = glu[t]
    for t in range(ls):
        acc = jnp.zeros(glu[0].shape, F32) + cb_ref[...]
        for w in range(CONV_WIDTH):
            j = t + w
            src = buf_ref[j] if j < nbuf else glu[j - nbuf]
            acc = acc + src * cw_ref[w:w + 1, :]
        act_ref[t] = _conv_post(acc, lg_ref[...], lb_ref[...])


def _conv_step(buf_t, z_t, cw, cb, lg, lb):
    ls, nseq, _ = z_t.shape
    nb = min(32, nseq)
    vec = _full((1, CONV_DIM))
    return pl.pallas_call(
        functools.partial(_conv_step_kernel, ls=ls),
        grid=(nseq // nb,),
        in_specs=[pl.BlockSpec((CONV_WIDTH - 1, nb, CONV_DIM), lambda i: (0, i, 0)),
                  pl.BlockSpec((ls, nb, 2 * CONV_DIM), lambda i: (0, i, 0)),
                  _full((CONV_WIDTH, CONV_DIM)), vec, vec, vec],
        out_specs=[pl.BlockSpec((ls, nb, CONV_DIM), lambda i: (0, i, 0)),
                   pl.BlockSpec((ls, nb, CONV_DIM), lambda i: (0, i, 0))],
        out_shape=[jax.ShapeDtypeStruct((ls, nseq, CONV_DIM), BF16),
                   jax.ShapeDtypeStruct((ls, nseq, CONV_DIM), F32)],
        compiler_params=_params(("parallel",)),
        name="conv_step",
    )(buf_t, z_t, cw, cb, lg, lb)


def _gla_levels(c, diag):
    levels, s = [], c // 2
    while s >= diag:
        levels.append(s)
        s //= 2
    return tuple(levels)


def _gla_sum_masks(c, levels):
    i = np.arange(c)[:, None]
    r = np.arange(c)[None, :]
    mats = [r <= i, r > i]
    for s in levels:
        ref = (i & ~(2 * s - 1)) + s - 1
        upper = (i & s) != 0
        mats.append(np.where(upper, (r > ref) & (r <= i), (r > i) & (r <= ref)))
    return np.concatenate(mats, axis=0).astype(np.float32)


def _gla_kernel(*refs, ns, **kw):
    z_ref, s0_ref, mall_ref, wg_ref, bg_ref, ng_ref, hsum_ref, act_ref, sout_ref, st_sc = refs
    ci = pl.program_id(1)

    @pl.when(ci == 0)
    def _():
        st_sc[...] = s0_ref[...]

    _gla_chunks(z_ref, act_ref, st_sc, mall_ref, wg_ref, bg_ref, ng_ref, hsum_ref, ns=ns, **kw)

    @pl.when(ci == pl.num_programs(1) - 1)
    def _():
        sout_ref[...] = st_sc[...]


def _gla_chunks(z_ref, act_ref, st_sc, mall_ref, wg_ref, bg_ref, ng_ref, hsum_ref, *, ns, c, diag, levels, valid, mmdt):
    seqs = range(ns)
    z = [z_ref[n] for n in seqs]
    q = [zz[:, 0:GLA_QK] * (GLA_DK ** -0.5) for zz in z]
    k = [zz[:, GLA_QK:2 * GLA_QK] for zz in z]
    v = [zz[:, 2 * GLA_QK:2 * GLA_QK + GLA_V] for zz in z]
    wg = wg_ref[...].astype(mmdt)
    x = [_dot(zz[:, 2 * GLA_QK + 2 * GLA_V:].astype(mmdt), wg) + bg_ref[...] for zz in z]
    g = [(jnp.minimum(xx, 0.0) - jnp.log(1.0 + jnp.exp(-jnp.abs(xx)))) * (1.0 / GLA_TAU) for xx in x]
    row = lax.broadcasted_iota(jnp.int32, (c, 1), 0)
    if valid < c:
        g = [jnp.where(row < valid, gg, 0.0) for gg in g]

    parts = []
    for gg in g:
        g1 = gg.astype(BF16)
        r1 = gg - g1.astype(F32)
        g2 = r1.astype(BF16)
        parts += [g1, g2, (r1 - g2.astype(F32)).astype(BF16)]
    e3 = _dot(mall_ref[...].astype(mmdt), jnp.concatenate(parts, axis=1).astype(mmdt))
    w = GLA_QK
    e = [e3[:, 3 * n * w:(3 * n + 1) * w] + e3[:, (3 * n + 1) * w:(3 * n + 2) * w] + e3[:, (3 * n + 2) * w:(3 * n + 3) * w]
         for n in seqs]
    b = [ee[0:c] for ee in e]
    qe = [(q[n] * jnp.exp(b[n])).astype(mmdt) for n in seqs]
    kd = [(k[n] * jnp.exp(e[n][c:2 * c])).astype(mmdt) for n in seqs]
    dlast = [jnp.exp(bb[c - 1:c, :]) for bb in b]
    vm = [vv.astype(mmdt) for vv in v]

    lvl = []
    if levels:
        ri = lax.broadcasted_iota(jnp.int32, (c, c), 0)
        rj = lax.broadcasted_iota(jnp.int32, (c, c), 1)
        for li, s in enumerate(levels):
            pair = ~(2 * s - 1)
            msk = ((ri & s) != 0) & ((rj & s) == 0) & ((ri & pair) == (rj & pair))
            dec = [jnp.exp(e[n][(2 + li) * c:(3 + li) * c]) for n in seqs]
            lvl.append(([(q[n] * dec[n]).astype(mmdt) for n in seqs], [(k[n] * dec[n]).astype(mmdt) for n in seqs], msk))

    dcol = [[jnp.broadcast_to(dlast[n][:, t * LANE:(t + 1) * LANE], (LANE, LANE)).T for t in range(GLA_QK // LANE)]
            for n in seqs]

    outs = [[] for _ in seqs]
    for h in range(GLA_HEADS):
        ks = slice(h * GLA_DK, (h + 1) * GLA_DK)
        vs = slice(h * GLA_DV, (h + 1) * GLA_DV)
        kt, ko = (h * GLA_DK) // LANE, (h * GLA_DK) % LANE
        st = [st_sc[n, h] for n in seqs]
        o_h = [_dot(qe[n][:, ks], st[n].astype(mmdt)) for n in seqs]
        if lvl:
            a = [jnp.zeros((c, c), F32) for _ in seqs]
            for ql, kl, msk in lvl:
                a = [a[n] + jnp.where(msk, _dot_nt(ql[n][:, ks], kl[n][:, ks]), 0.0) for n in seqs]
            o_h = [o_h[n] + _dot(a[n].astype(mmdt), vm[n][:, vs]) for n in seqs]
        for n in seqs:
            st_sc[n, h] = st[n] * dcol[n][kt][ko:ko + GLA_DK, :] + _dot_tn(kd[n][:, ks], vm[n][:, vs])
            outs[n].append(o_h[n])
    o = [jnp.concatenate(outs[n], axis=1) for n in seqs]

    hsum = hsum_ref[...].astype(mmdt)
    for d in range(min(diag, valid)):
        if d == 0:
            t = [q[n] * k[n] for n in seqs]
            vsh = v
        else:
            ok = (row & (diag - 1)) >= d
            dec = [jnp.exp(jnp.where(ok, b[n] - pltpu.roll(b[n], d, 0), 0.0)) for n in seqs]
            t = [jnp.where(ok, q[n] * pltpu.roll(k[n], d, 0) * dec[n], 0.0) for n in seqs]
            vsh = [pltpu.roll(v[n], d, 0) for n in seqs]
        o = [o[n] + _dot(t[n].astype(mmdt), hsum) * vsh[n] for n in seqs]

    ng = ng_ref[...]
    for n in seqs:
        normed = [_rms(o[n][:, h * GLA_DV:(h + 1) * GLA_DV], ng[:, h * GLA_DV:(h + 1) * GLA_DV])
                  for h in range(GLA_HEADS)]
        r = z[n][:, 2 * GLA_QK + GLA_V:2 * GLA_QK + 2 * GLA_V]
        act_ref[n] = (jnp.concatenate(normed, axis=1) * (r * jax.nn.sigmoid(r))).astype(act_ref.dtype)


def _gla(z, s0, layer, wg, bg, ng, nb, sl, c, valid, act_dtype, ns):
    assert nb % ns == 0 and sl % c == 0 and GLA_DV == LANE
    diag = min(GLA_DIAG, c)
    levels = _gla_levels(c, diag)
    mall = jnp.asarray(_gla_sum_masks(c, levels), BF16)
    hs = np.zeros((GLA_QK, GLA_V), np.float32)
    for h in range(GLA_HEADS):
        hs[h * GLA_DK:(h + 1) * GLA_DK, h * GLA_DV:(h + 1) * GLA_DV] = 1.0
    hsum = jnp.asarray(hs, BF16)
    nc = sl // c
    mmdt = BF16 if c % 16 == 0 else F32
    st_shape = (ns, GLA_HEADS, GLA_DK, GLA_DV)
    rows = lambda width: pl.BlockSpec((None, ns, c, width), lambda b, i: (b, 0, i, 0))
    act, st = pl.pallas_call(
        functools.partial(_gla_kernel, ns=ns, c=c, diag=diag, levels=levels, valid=valid, mmdt=mmdt),
        grid=(nb // ns, nc),
        in_specs=[rows(Z_GLA), pl.BlockSpec((None,) + st_shape, lambda b, i: (layer, b, 0, 0, 0)),
                  _full(mall.shape), _full(wg.shape), _full(bg.shape), _full(ng.shape), _full(hsum.shape)],
        out_specs=[rows(GLA_V), pl.BlockSpec(st_shape, lambda b, i: (b, 0, 0, 0))],
        out_shape=[jax.ShapeDtypeStruct((nb // ns, ns, sl, GLA_V), act_dtype),
                   jax.ShapeDtypeStruct((nb, GLA_HEADS, GLA_DK, GLA_DV), F32)],
        scratch_shapes=[pltpu.VMEM(st_shape, F32)],
        compiler_params=_params(("parallel", "arbitrary")),
        name="gla",
    )(z.reshape(nb // ns, ns, sl, Z_GLA), s0, mall, wg, bg, ng, hsum)
    return act.reshape(nb * sl, GLA_V), st


def _gmlp_kernel(z_ref, ws_ref, bsb_ref, lg_ref, lb_ref, act_ref, *v_refs, nchunk):
    ti = lax.broadcasted_iota(jnp.int32, (GMLP_CHUNK, GMLP_CHUNK), 0)
    tj = lax.broadcasted_iota(jnp.int32, (GMLP_CHUNK, GMLP_CHUNK), 1)
    wt = [jnp.where(ti >= tj, ws_ref[g], 0.0).astype(BF16) for g in range(GMLP_GROUPS)]
    gc = GMLP_DIM // GMLP_GROUPS
    for ch in range(nchunk):
        rows = slice(ch * GMLP_CHUNK, (ch + 1) * GMLP_CHUNK)
        z = z_ref[rows, :]
        ge = z * (0.5 * (1.0 + jnp.tanh(math.sqrt(2.0 / math.pi) * (z + 0.044715 * (z * z * z)))))
        v = _ln(ge[:, GMLP_DIM:], lg_ref[...], lb_ref[...])
        if v_refs:
            v_refs[0][rows, :] = v
        vb = v.astype(BF16)
        s = jnp.concatenate([_dot(wt[g], vb[:, g * gc:(g + 1) * gc]) for g in range(GMLP_GROUPS)], axis=1)
        act_ref[rows, :] = (ge[:, :GMLP_DIM] * (s + bsb_ref[...])).astype(BF16)


def _gmlp(z, ws, bsb, lg, lb, emit_v):
    m = z.shape[0]
    tm = min(512, m)
    assert tm % GMLP_CHUNK == 0 and m % tm == 0
    blk = lambda n: pl.BlockSpec((tm, n), lambda i: (i, 0))
    vec = _full((1, GMLP_DIM))
    out_specs = [blk(GMLP_DIM)]
    out_shape = [jax.ShapeDtypeStruct((m, GMLP_DIM), BF16)]
    if emit_v:
        out_specs.append(blk(GMLP_DIM))
        out_shape.append(jax.ShapeDtypeStruct((m, GMLP_DIM), F32))
    return pl.pallas_call(
        functools.partial(_gmlp_kernel, nchunk=tm // GMLP_CHUNK),
        grid=(m // tm,),
        in_specs=[blk(2 * GMLP_DIM), _full(ws.shape), _full(bsb.shape), vec, vec],
        out_specs=out_specs,
        out_shape=out_shape,
        compiler_params=_params(("parallel",)),
        name="gmlp",
    )(z, ws, bsb, lg, lb)


def _mla_latents(z, ck, sk, gkv, lat_ref, kr_ref):
    c = _rms(z[:, MLA_Q_RANK:MLA_Q_RANK + MLA_KV_RANK], gkv)
    lat_ref[...] = c
    o = MLA_Q_RANK + MLA_KV_RANK
    kr = z[:, o:o + LANE] * ck + z[:, o + LANE:o + 2 * LANE] * sk
    kr_ref[...] = kr[:, :MLA_ROPE]
    return c, kr


def _mla_proj_kernel(z_ref, cq_ref, sq_ref, ck_ref, sk_ref, gq_ref, gkv_ref, wq1_ref, wq2_ref, wuk_ref,
                     wuv_ref, sel_ref, q_ref, k_ref, v_ref, lat_ref, kr_ref):
    z = z_ref[...]
    hq = _rms(z[:, :MLA_Q_RANK], gq_ref[...]).astype(BF16)
    q1 = _dot(hq, wq1_ref[...])
    q2 = _dot(hq, wq2_ref[...])
    cosq = cq_ref[...]
    sinq = sq_ref[...]
    for h in range(MLA_HEADS):
        hs = slice(h * LANE, (h + 1) * LANE)
        q_ref[:, hs] = ((q1[:, hs] * cosq + q2[:, hs] * sinq) * MLA_SCALE).astype(BF16)
    c, kr = _mla_latents(z, ck_ref[...], sk_ref[...], gkv_ref[...], lat_ref, kr_ref)
    cb = c.astype(BF16)
    k_ref[...] = (_dot(cb, wuk_ref[...]) + _dot(kr.astype(BF16), sel_ref[...])).astype(BF16)
    v_ref[...] = _dot(cb, wuv_ref[...]).astype(BF16)


def _mla_proj(z, tabs, gq, gkv, wq1, wq2, wuk, wuv, sel, sl):
    m = z.shape[0]
    tm = min(512, sl)
    nl = sl // tm
    row = lambda n: pl.BlockSpec((tm, n), lambda i: (i, 0))
    tab = pl.BlockSpec((tm, LANE), lambda i: (i % nl, 0))
    hd = MLA_HEADS * LANE
    return pl.pallas_call(
        _mla_proj_kernel,
        grid=(m // tm,),
        in_specs=[row(Z_MLA), tab, tab, tab, tab, _full(gq.shape), _full(gkv.shape), _full(wq1.shape),
                  _full(wq2.shape), _full(wuk.shape), _full(wuv.shape), _full(sel.shape)],
        out_specs=[row(hd), row(hd), row(MLA_HEADS * MLA_V), row(MLA_KV_RANK), row(MLA_ROPE)],
        out_shape=[jax.ShapeDtypeStruct((m, hd), BF16), jax.ShapeDtypeStruct((m, hd), BF16),
                   jax.ShapeDtypeStruct((m, MLA_HEADS * MLA_V), BF16),
                   jax.ShapeDtypeStruct((m, MLA_KV_RANK), F32), jax.ShapeDtypeStruct((m, MLA_ROPE), F32)],
        compiler_params=_params(("parallel",)),
        name="mla_proj",
    )(z, *tabs, gq, gkv, wq1, wq2, wuk, wuv, sel)


def _flash_kernel(q_ref, k_ref, v_ref, o_ref, m_sc, acc_sc, *, t, tk):
    qi = pl.program_id(2)
    m_sc[...] = jnp.full(m_sc.shape, -jnp.inf, F32)
    acc_sc[...] = jnp.zeros(acc_sc.shape, F32)
    ones = jnp.ones((tk, LANE), BF16)

    def block(j, r0, nr, triangular):
        rows = slice(r0, r0 + nr)
        start = pl.multiple_of(j * tk, tk)
        kblk = k_ref[pl.ds(start, tk), :]
        vext = jnp.concatenate([v_ref[pl.ds(start, tk), :], ones], axis=1)
        heads = range(2)
        s = [_dot_nt(q_ref[rows, a * LANE:(a + 1) * LANE], kblk[:, a * LANE:(a + 1) * LANE]) for a in heads]
        if triangular:
            ri = lax.broadcasted_iota(jnp.int32, (nr, tk), 0)
            rj = lax.broadcasted_iota(jnp.int32, (nr, tk), 1)
            s = [jnp.where(ri >= rj, ss, -jnp.inf) for ss in s]
        m_prev = [m_sc[a, rows] for a in heads]
        m_new = [jnp.maximum(m_prev[a], jnp.max(s[a], axis=-1, keepdims=True)) for a in heads]
        alpha = [jnp.exp(m_prev[a] - m_new[a]) for a in heads]
        p = [jnp.exp(s[a] - jnp.concatenate([m_new[a]] * (tk // LANE), axis=1)).astype(BF16) for a in heads]
        pv = [_dot(p[a], vext) for a in heads]
        for a in heads:
            acc_sc[a, rows] = jnp.concatenate([alpha[a]] * 2, axis=1) * acc_sc[a, rows] + pv[a]
            m_sc[a, rows] = m_new[a]

    per = t // tk
    nfull = qi * per

    def body(j, carry):
        block(j, 0, t, False)
        return carry

    lax.fori_loop(0, nfull, body, 0)
    for d in range(per):
        block(nfull + d, d * tk, tk, True)
        if (d + 1) * tk < t:
            block(nfull + d, (d + 1) * tk, t - (d + 1) * tk, False)
    lane = lax.broadcasted_iota(jnp.int32, (t, LANE), 1)
    a0, a1 = acc_sc[0], acc_sc[1]
    o_ref[...] = jnp.where(lane < MLA_V, a0[:, :LANE] / a0[:, LANE:], a1[:, :LANE] / a1[:, LANE:]).astype(BF16)


def _flash(q, k, v, nb, sl):
    t = min(1024, sl)
    tk = min(512, t)
    nt = sl // t
    return pl.pallas_call(
        functools.partial(_flash_kernel, t=t, tk=tk),
        grid=(nb, MLA_HEADS // 2, nt),
        in_specs=[pl.BlockSpec((t, 2 * LANE), lambda b, h, i: (b * nt + i, h)),
                  pl.BlockSpec((sl, 2 * LANE), lambda b, h, i: (b, h)),
                  pl.BlockSpec((sl, LANE), lambda b, h, i: (b, h))],
        out_specs=pl.BlockSpec((t, LANE), lambda b, h, i: (b * nt + i, h)),
        out_shape=jax.ShapeDtypeStruct((nb * sl, MLA_HEADS * MLA_V), BF16),
        scratch_shapes=[pltpu.VMEM((2, t, LANE), F32), pltpu.VMEM((2, t, 2 * LANE), F32)],
        compiler_params=_params(("parallel", "parallel", "arbitrary")),
        name="mla_flash",
    )(q, k, v)


def _mla_proj_step_kernel(z_ref, c8_ref, s8_ref, ck_ref, sk_ref, gq_ref, gkv_ref, wqn_ref, wqr_ref, wqs_ref,
                          wukt_ref, qlat_ref, qr_ref, lat_ref, kr_ref):
    z = z_ref[...]
    hq = _rms(z[:, :MLA_Q_RANK], gq_ref[...]).astype(BF16)
    qn = _dot(hq, wqn_ref[...]).astype(BF16)
    for h in range(MLA_HEADS):
        lat = _dot(qn[:, h * LANE:(h + 1) * LANE], wukt_ref[h])
        qlat_ref[:, h * MLA_KV_RANK:(h + 1) * MLA_KV_RANK] = (lat * MLA_SCALE).astype(BF16)
    qr = _dot(hq, wqr_ref[...]) * c8_ref[...] + _dot(hq, wqs_ref[...]) * s8_ref[...]
    qr_ref[...] = (qr * MLA_SCALE).astype(BF16)
    _mla_latents(z, ck_ref[...], sk_ref[...], gkv_ref[...], lat_ref, kr_ref)


def _mla_proj_step(z, tabs, gq, gkv, wqn, wqr, wqs, wukt):
    m = z.shape[0]
    args = (z, *tabs, gq, gkv, wqn, wqr, wqs, wukt)
    return pl.pallas_call(
        _mla_proj_step_kernel,
        grid=(1,),
        in_specs=[_full(a.shape) for a in args],
        out_specs=[_full((m, MLA_HEADS * MLA_KV_RANK)), _full((m, MLA_HEADS * MLA_ROPE)),
                   _full((m, MLA_KV_RANK)), _full((m, MLA_ROPE))],
        out_shape=[jax.ShapeDtypeStruct((m, MLA_HEADS * MLA_KV_RANK), BF16),
                   jax.ShapeDtypeStruct((m, MLA_HEADS * MLA_ROPE), BF16),
                   jax.ShapeDtypeStruct((m, MLA_KV_RANK), F32), jax.ShapeDtypeStruct((m, MLA_ROPE), F32)],
        compiler_params=_params(("arbitrary",)),
        name="mla_proj_step",
    )(*args)


def _paged_kernel(pt_ref, ql_ref, qr_ref, cn_ref, rn_ref, lat_hbm, krt_hbm, o_ref,
                  lat_buf, kr_buf, sem, m_sc, l_sc, acc_sc, kb_sc, *, layer, pp, ls, npad, nsq):
    j = pl.program_id(1)
    nj = pl.num_programs(1)
    step = pl.program_id(0) * nj + j
    nsteps = pl.num_programs(0) * nj
    slot = lax.rem(step, 2)
    seqs = range(nsq)

    def page_copies(at_step, sl):
        group = at_step // nj
        first_page = (at_step - group * nj) * pp
        out = []
        for u in seqs:
            base = (group * nsq + u) * (nj * pp) + first_page
            for i in range(pp):
                pg = pt_ref[base + i]
                out.append(pltpu.make_async_copy(lat_hbm.at[layer, pg], lat_buf.at[sl, u * pp + i], sem.at[0, sl]))
                out.append(pltpu.make_async_copy(krt_hbm.at[layer, pg], kr_buf.at[sl, u * pp + i], sem.at[1, sl]))
        return out

    @pl.when(step == 0)
    def _():
        for cp in page_copies(step, slot):
            cp.start()

    @pl.when(step + 1 < nsteps)
    def _():
        for cp in page_copies(step + 1, 1 - slot):
            cp.start()

    nrow = ql_ref.shape[1]

    @pl.when(j == 0)
    def _():
        for u in seqs:
            cn = cn_ref[u]
            s = _dot_nt(ql_ref[u].astype(F32), cn) + _dot_nt(qr_ref[u].astype(F32), rn_ref[u])
            tok = jnp.right_shift(lax.broadcasted_iota(jnp.int32, (nrow, npad), 0), int(math.log2(MLA_HEADS)))
            key = lax.broadcasted_iota(jnp.int32, (nrow, npad), 1)
            s = jnp.where((key <= tok) & (key < ls), s, -jnp.inf)
            m = jnp.max(s, axis=-1, keepdims=True)
            p = jnp.exp(s - m)
            m_sc[u] = m
            l_sc[u] = jnp.sum(p, axis=-1, keepdims=True)
            acc_sc[u] = _dot(p, cn)

    for cp in page_copies(step, slot):
        cp.wait()
    for u in seqs:
        for i in range(pp):
            kb_sc[u, i * LANE:(i + 1) * LANE, :] = lat_buf[slot, u * pp + i].astype(BF16)
    kb = [kb_sc[u] for u in seqs]
    rb = [jnp.concatenate([kr_buf[slot, u * pp + i] for i in range(pp)], axis=1) for u in seqs]
    s = [_dot_nt(ql_ref[u], kb[u]) + _dot(qr_ref[u].astype(F32), rb[u]) for u in seqs]
    m_prev = [m_sc[u] for u in seqs]
    m_new = [jnp.maximum(m_prev[u], jnp.max(s[u], axis=-1, keepdims=True)) for u in seqs]
    alpha = [jnp.exp(m_prev[u] - m_new[u]) for u in seqs]
    p = [jnp.exp(s[u] - m_new[u]) for u in seqs]
    pv = [_dot(p[u].astype(BF16), kb[u]) for u in seqs]
    for u in seqs:
        l_sc[u] = alpha[u] * l_sc[u] + jnp.sum(p[u], axis=-1, keepdims=True)
        acc_sc[u] = alpha[u] * acc_sc[u] + pv[u]
        m_sc[u] = m_new[u]

    @pl.when(j == nj - 1)
    def _():
        for u in seqs:
            o_ref[u] = (acc_sc[u] / l_sc[u]).astype(BF16)


def _paged_attention(pt_flat, ql, qr, cn, rn, cache_lat, cache_krt, layer, n_pages, ls):
    nseq, nrow, _ = ql.shape
    page = cache_lat.shape[2]
    assert page == LANE
    pp = min(32, n_pages)
    nsq = 2 if nseq % 2 == 0 else 1
    assert n_pages % pp == 0
    npad = cn.shape[1]
    seq = lambda shp: pl.BlockSpec((nsq,) + shp, lambda b, j, pt: (b, 0, 0))
    hbm = pl.BlockSpec(memory_space=pl.ANY)
    grid_spec = pltpu.PrefetchScalarGridSpec(
        num_scalar_prefetch=1,
        grid=(nseq // nsq, n_pages // pp),
        in_specs=[seq((nrow, MLA_KV_RANK)), seq((nrow, MLA_ROPE)), seq((npad, MLA_KV_RANK)),
                  seq((npad, MLA_ROPE)), hbm, hbm],
        out_specs=seq((nrow, MLA_KV_RANK)),
        scratch_shapes=[pltpu.VMEM((2, nsq * pp, page, MLA_KV_RANK), F32),
                        pltpu.VMEM((2, nsq * pp, MLA_ROPE, page), F32),
                        pltpu.SemaphoreType.DMA((2, 2)),
                        pltpu.VMEM((nsq, nrow, 1), F32), pltpu.VMEM((nsq, nrow, 1), F32),
                        pltpu.VMEM((nsq, nrow, MLA_KV_RANK), F32),
                        pltpu.VMEM((nsq, pp * page, MLA_KV_RANK), BF16)],
    )
    return pl.pallas_call(
        functools.partial(_paged_kernel, layer=layer, pp=pp, ls=ls, npad=npad, nsq=nsq),
        grid_spec=grid_spec,
        out_shape=jax.ShapeDtypeStruct((nseq, nrow, MLA_KV_RANK), BF16),
        compiler_params=_params(("arbitrary", "arbitrary")),
        name="mla_paged",
    )(pt_flat, ql, qr, cn, rn, cache_lat, cache_krt)


def _mla_up_kernel(o_ref, w_ref, out_ref):
    acc = _dot(o_ref[:, 0:MLA_KV_RANK], w_ref[0])
    for h in range(1, MLA_HEADS):
        acc = acc + _dot(o_ref[:, h * MLA_KV_RANK:(h + 1) * MLA_KV_RANK], w_ref[h])
    out_ref[...] = acc.astype(BF16)


def _mla_up(o_lat, wuvp):
    m = o_lat.shape[0]
    return pl.pallas_call(
        _mla_up_kernel,
        grid=(1,),
        in_specs=[_full(o_lat.shape), _full(wuvp.shape)],
        out_specs=_full((m, MLA_HEADS * MLA_V)),
        out_shape=jax.ShapeDtypeStruct((m, MLA_HEADS * MLA_V), BF16),
        compiler_params=_params(("arbitrary",)),
        name="mla_up",
    )(o_lat, wuvp)


def _merge_kernel(x_ref, a_ref, b_ref, c_ref, d_ref, g_ref, wg_ref, wa_ref, wb_ref, wc_ref, wd_ref, wo_ref,
                  gp_ref, o_ref):
    x = x_ref[...]
    h = _rms(x, g_ref[...]).astype(BF16)
    merged = None
    for i, (act, w) in enumerate(((a_ref, wa_ref), (b_ref, wb_ref), (c_ref, wc_ref), (d_ref, wd_ref))):
        gate = jax.nn.sigmoid(_dot(h, wg_ref[:, i * D_MODEL:(i + 1) * D_MODEL]))
        term = gate * _dot(act[...], w[...])
        merged = term if merged is None else merged + term
    y = _dot(merged.astype(BF16), wo_ref[...])
    o_ref[...] = x + _rms(y, gp_ref[...])


def _merge(x, acts, g, wgate, wouts, wo, gp):
    m = x.shape[0]
    tm = min(256, m)
    row = lambda n: pl.BlockSpec((tm, n), lambda i: (i, 0))
    vec = _full((1, D_MODEL))
    return pl.pallas_call(
        _merge_kernel,
        grid=(m // tm,),
        in_specs=[row(D_MODEL)] + [row(a.shape[1]) for a in acts] + [vec, _full(wgate.shape)]
        + [_full(w.shape) for w in wouts] + [_full(wo.shape), vec],
        out_specs=row(D_MODEL),
        out_shape=jax.ShapeDtypeStruct((m, D_MODEL), F32),
        compiler_params=_params(("parallel",)),
        name="merge",
    )(x, *acts, g, wgate, *wouts, wo, gp)


def _ffn_kernel(x_ref, g_ref, w1_ref, w2_ref, gp_ref, o_ref, h_sc, acc_sc):
    j = pl.program_id(1)

    @pl.when(j == 0)
    def _():
        h_sc[...] = _rms(x_ref[...], g_ref[...]).astype(BF16)
        acc_sc[...] = jnp.zeros(acc_sc.shape, F32)

    t = jnp.maximum(_dot(h_sc[...], w1_ref[...]), 0.0)
    acc_sc[...] += _dot((t * t).astype(BF16), w2_ref[...])

    @pl.when(j == pl.num_programs(1) - 1)
    def _():
        o_ref[...] = x_ref[...] + _rms(acc_sc[...], gp_ref[...])


def _ffn(x, g, w1, w2, gp):
    m = x.shape[0]
    tm = min(1024, m)
    tf = 1024
    vec = _full((1, D_MODEL))
    return pl.pallas_call(
        _ffn_kernel,
        grid=(m // tm, D_FF // tf),
        in_specs=[pl.BlockSpec((tm, D_MODEL), lambda i, j: (i, 0)), vec,
                  pl.BlockSpec((D_MODEL, tf), lambda i, j: (0, j)),
                  pl.BlockSpec((tf, D_MODEL), lambda i, j: (j, 0)), vec],
        out_specs=pl.BlockSpec((tm, D_MODEL), lambda i, j: (i, 0)),
        out_shape=jax.ShapeDtypeStruct((m, D_MODEL), F32),
        scratch_shapes=[pltpu.VMEM((tm, D_MODEL), BF16), pltpu.VMEM((tm, D_MODEL), F32)],
        compiler_params=_params(("parallel", "arbitrary")),
        name="ffn",
    )(x, g, w1, w2, gp)


def _swap_halves(w):
    half = w.shape[-1] // 2
    return jnp.concatenate([w[..., half:], w[..., :half]], axis=-1)


def _prep_layer(p):
    w_in = p['w_in']
    o_gla = 2 * CONV_DIM
    o_gd = o_gla + 2 * GLA_QK + GLA_V
    o_r = o_gd + GLA_GATE_RANK
    o_gmlp = o_r + GLA_V
    o_mla = o_gmlp + 2 * GMLP_DIM
    o_kr = o_mla + MLA_Q_RANK + MLA_KV_RANK
    o_gate = o_kr + MLA_ROPE
    rows = w_in.shape[0]
    zpad = lambda n: jnp.zeros((rows, n), w_in.dtype)
    w_kr = w_in[:, o_kr:o_gate]
    q = {}
    q['w_conv'] = w_in[:, :o_gla].astype(BF16)
    q['w_gla'] = jnp.concatenate([w_in[:, o_gla:o_gd], w_in[:, o_r:o_gmlp], w_in[:, o_gd:o_r],
                                  zpad(LANE - GLA_GATE_RANK)], axis=1).astype(BF16)
    q['w_gmlp'] = w_in[:, o_gmlp:o_mla].astype(BF16)
    q['w_mla'] = jnp.concatenate([w_in[:, o_mla:o_kr], w_kr, zpad(LANE - MLA_ROPE), _swap_halves(w_kr),
                                  zpad(LANE - MLA_ROPE)], axis=1).astype(BF16)
    q['w_gate'] = w_in[:, o_gate:].astype(BF16)

    q['gla_wg'] = jnp.concatenate([p['gla_gate_up'], jnp.zeros((LANE - GLA_GATE_RANK, GLA_QK), F32)],
                                  axis=0).astype(BF16)
    q['gla_bg'] = p['gla_gate_b'].reshape(1, GLA_QK)
    q['gla_ng'] = p['gla_norm_g'].reshape(1, GLA_V)

    rq = MLA_Q_RANK
    wq = p['mla_w_uq'].reshape(rq, MLA_HEADS, MLA_NOPE + MLA_ROPE)
    nope, rope = wq[..., :MLA_NOPE], wq[..., MLA_NOPE:]
    rope_sw = _swap_halves(rope)
    zq = lambda n: jnp.zeros((rq, MLA_HEADS, n), F32)
    hd = MLA_HEADS * LANE
    q['wq1'] = jnp.concatenate([nope, rope, zq(LANE - MLA_NOPE - MLA_ROPE)], -1).reshape(rq, hd).astype(BF16)
    q['wq2'] = jnp.concatenate([zq(MLA_NOPE), rope_sw, zq(LANE - MLA_NOPE - MLA_ROPE)], -1).reshape(rq, hd).astype(BF16)
    q['wqn'] = jnp.concatenate([nope, zq(LANE - MLA_NOPE)], -1).reshape(rq, hd).astype(BF16)
    q['wqr'] = rope.reshape(rq, MLA_HEADS * MLA_ROPE).astype(BF16)
    q['wqs'] = rope_sw.reshape(rq, MLA_HEADS * MLA_ROPE).astype(BF16)
    wuk, wuv = p['mla_w_uk'], p['mla_w_uv']
    zk = jnp.zeros((MLA_KV_RANK, MLA_HEADS, LANE - MLA_NOPE), F32)
    q['wuk'] = jnp.concatenate([wuk, zk], -1).reshape(MLA_KV_RANK, hd).astype(BF16)
    q['wuv'] = wuv.reshape(MLA_KV_RANK, MLA_HEADS * MLA_V).astype(BF16)
    q['wukt'] = jnp.concatenate([wuk, zk], -1).transpose(1, 2, 0).astype(BF16)
    eye = jnp.eye(MLA_HEADS, dtype=F32)
    q['wuvp'] = (wuv.transpose(1, 0, 2)[:, :, None, :] * eye[:, None, :, None]).reshape(
        MLA_HEADS, MLA_KV_RANK, MLA_HEADS * MLA_V).astype(BF16)

    for name in ('conv_out', 'gla_out', 'gmlp_out', 'mla_out', 'w_o', 'ffn_w1', 'ffn_w2'):
        q[name] = p[name].astype(BF16)
    for name in ('g_pre_mix', 'g_post_mix', 'g_pre_ffn', 'g_post_ffn', 'conv_b', 'conv_ln_g', 'conv_ln_b',
                 'gmlp_ln_g', 'gmlp_ln_b', 'mla_q_norm_g', 'mla_kv_norm_g'):
        q[name] = p[name].reshape(1, -1)
    q['conv_w'] = p['conv_w']
    q['gmlp_ws'] = p['gmlp_ws']
    q['gmlp_bs'] = p['gmlp_bs']
    return q


def _rope_tables(pos):
    half = MLA_ROPE // 2
    inv = ROPE_THETA ** (-jnp.arange(half, dtype=F32) / half)
    ang = pos.astype(F32)[:, None] * inv[None, :]
    cos, sin = jnp.cos(ang), jnp.sin(ang)
    return jnp.concatenate([cos, cos], -1), jnp.concatenate([-sin, sin], -1)


def _key_sel():
    sel = np.zeros((LANE, MLA_HEADS * LANE), np.float32)
    for h in range(MLA_HEADS):
        for j in range(MLA_ROPE):
            sel[j, h * LANE + MLA_NOPE + j] = 1.0
    return jnp.asarray(sel, BF16)


def _pad_lanes(t, left, total):
    n = t.shape[0]
    return jnp.concatenate([jnp.zeros((n, left), F32), t, jnp.zeros((n, total - left - t.shape[1]), F32)], -1)


def _mix_tail(x, acts, q):
    x = _merge(x, acts, q['g_pre_mix'], q['w_gate'],
               (q['conv_out'], q['gla_out'], q['gmlp_out'], q['mla_out']), q['w_o'], q['g_post_mix'])
    return _ffn(x, q['g_pre_ffn'], q['ffn_w1'], q['ffn_w2'], q['g_post_ffn'])


def _layer_prompt(x, q, tabs, sel, nb, sl):
    m = nb * sl
    z_conv, z_gla, z_gmlp, z_mla = _inproj(x, q['g_pre_mix'], (q['w_conv'], q['w_gla'], q['w_gmlp'], q['w_mla']),
                                           min(256, m))
    act_a, conv_st = _conv_prompt(z_conv, q['conv_w'], q['conv_b'], q['conv_ln_g'], q['conv_ln_b'], nb, sl)
    c = min(GLA_CHUNK, sl)
    s0 = jnp.zeros((1, nb, GLA_HEADS, GLA_DK, GLA_DV), F32)
    act_b, st = _gla(z_gla, s0, 0, q['gla_wg'], q['gla_bg'], q['gla_ng'], nb, sl, c, c, BF16,
                     GLA_SEQS_PROMPT if nb % GLA_SEQS_PROMPT == 0 else 1)
    bsb = jnp.repeat(q['gmlp_bs'].T, GMLP_DIM // GMLP_GROUPS, axis=1)
    (act_c,) = _gmlp(z_gmlp, q['gmlp_ws'], bsb, q['gmlp_ln_g'], q['gmlp_ln_b'], False)
    qp, kp, v, lat, kr = _mla_proj(z_mla, tabs, q['mla_q_norm_g'], q['mla_kv_norm_g'], q['wq1'], q['wq2'],
                                   q['wuk'], q['wuv'], sel, sl)
    act_d = _flash(qp, kp, v, nb, sl)
    x = _mix_tail(x, (act_a, act_b, act_c, act_d), q)
    return x, lat, kr, conv_st, st


def _layer_sample(x, q, tabs, nseq, ls, conv_buf, gla_states, cache_lat, cache_kr, pt_flat, layer, n_pages):
    m = nseq * ls
    z_conv, z_gla, z_gmlp, z_mla = _inproj(x, q['g_pre_mix'], (q['w_conv'], q['w_gla'], q['w_gmlp'], q['w_mla']),
                                           min(256, m))
    act_t, glu_t = _conv_step(conv_buf.transpose(1, 0, 2), z_conv.reshape(nseq, ls, -1).transpose(1, 0, 2),
                              q['conv_w'], q['conv_b'], q['conv_ln_g'], q['conv_ln_b'])
    act_a = act_t.transpose(1, 0, 2).reshape(m, CONV_DIM)
    conv_st = jnp.concatenate([conv_buf, glu_t.transpose(1, 0, 2)], axis=1)[:, -(CONV_WIDTH - 1):]

    lp = 8
    pad = lambda z: jnp.pad(z.reshape(nseq, ls, -1), ((0, 0), (0, lp - ls), (0, 0))).reshape(nseq * lp, -1)
    unpad = lambda a: a.reshape(nseq, lp, -1)[:, :ls].reshape(m, -1)
    act_b, st = _gla(pad(z_gla), gla_states, layer, q['gla_wg'], q['gla_bg'], q['gla_ng'], nseq, lp, lp, ls, F32,
                     GLA_SEQS_STEP if nseq % GLA_SEQS_STEP == 0 else 1)
    act_b = unpad(act_b).astype(BF16)

    per = GMLP_CHUNK // lp
    eye = jnp.eye(per, dtype=F32)
    wk = (eye[None, :, None, :, None] * q['gmlp_ws'][:, None, :lp, None, :lp]).reshape(
        GMLP_GROUPS, GMLP_CHUNK, GMLP_CHUNK)
    bsb = jnp.repeat(jnp.tile(q['gmlp_bs'][:, :lp], (1, per)).T, GMLP_DIM // GMLP_GROUPS, axis=1)
    act_c, v_rows = _gmlp(pad(z_gmlp), wk, bsb, q['gmlp_ln_g'], q['gmlp_ln_b'], True)
    act_c, v_rows = unpad(act_c), unpad(v_rows)

    ql, qr, lat, kr = _mla_proj_step(z_mla, tabs, q['mla_q_norm_g'], q['mla_kv_norm_g'], q['wqn'], q['wqr'],
                                     q['wqs'], q['wukt'])
    npad = 8
    padk = lambda t: jnp.pad(t.reshape(nseq, ls, -1), ((0, 0), (0, npad - ls), (0, 0)))
    o_lat = _paged_attention(pt_flat, ql.reshape(nseq, ls * MLA_HEADS, MLA_KV_RANK),
                             qr.reshape(nseq, ls * MLA_HEADS, MLA_ROPE), padk(lat), padk(kr),
                             cache_lat, cache_kr, layer, n_pages, ls)
    act_d = _mla_up(o_lat.reshape(m, MLA_HEADS * MLA_KV_RANK), q['wuvp'])
    x = _mix_tail(x, (act_a, act_b, act_c, act_d), q)
    return x, lat, kr, conv_st, st, v_rows


def kernel(x_prompt, x_sample, cache_mla_latent, cache_mla_krope, state_conv, state_gla, page_table,
           g_pre_mix, w_in, conv_w, conv_b, conv_ln_g, conv_ln_b, conv_out,
           gla_gate_up, gla_gate_b, gla_norm_g, gla_out,
           gmlp_ln_g, gmlp_ln_b, gmlp_ws, gmlp_bs, gmlp_out,
           mla_q_norm_g, mla_w_uq, mla_kv_norm_g, mla_w_uk, mla_w_uv, mla_out,
           w_o, g_post_mix, g_pre_ffn, ffn_w1, ffn_w2, g_post_ffn):
    params = {'g_pre_mix': g_pre_mix, 'w_in': w_in, 'conv_w': conv_w, 'conv_b': conv_b, 'conv_ln_g': conv_ln_g,
              'conv_ln_b': conv_ln_b, 'conv_out': conv_out, 'gla_gate_up': gla_gate_up, 'gla_gate_b': gla_gate_b,
              'gla_norm_g': gla_norm_g, 'gla_out': gla_out, 'gmlp_ln_g': gmlp_ln_g, 'gmlp_ln_b': gmlp_ln_b,
              'gmlp_ws': gmlp_ws, 'gmlp_bs': gmlp_bs, 'gmlp_out': gmlp_out, 'mla_q_norm_g': mla_q_norm_g,
              'mla_w_uq': mla_w_uq, 'mla_kv_norm_g': mla_kv_norm_g, 'mla_w_uk': mla_w_uk, 'mla_w_uv': mla_w_uv,
              'mla_out': mla_out, 'w_o': w_o, 'g_post_mix': g_post_mix, 'g_pre_ffn': g_pre_ffn,
              'ffn_w1': ffn_w1, 'ffn_w2': ffn_w2, 'g_post_ffn': g_post_ffn}
    depth = w_in.shape[0]
    nb, sl, _ = x_prompt.shape
    nseq, ls, _ = x_sample.shape
    n_pages = page_table.shape[1]
    past_len = n_pages * cache_mla_latent.shape[2]

    cos_p, sin_p = _rope_tables(jnp.arange(sl, dtype=jnp.int32))
    tabs_p = (jnp.concatenate([jnp.ones((sl, MLA_NOPE), F32), cos_p, jnp.zeros((sl, LANE - MLA_NOPE - MLA_ROPE), F32)], -1),
              _pad_lanes(sin_p, MLA_NOPE, LANE), _pad_lanes(cos_p, 0, LANE), _pad_lanes(sin_p, 0, LANE))
    cos_s, sin_s = _rope_tables(past_len + jnp.arange(ls, dtype=jnp.int32))
    cos_s, sin_s = jnp.tile(cos_s, (nseq, 1)), jnp.tile(sin_s, (nseq, 1))
    tabs_s = (jnp.tile(cos_s, (1, MLA_HEADS)), jnp.tile(sin_s, (1, MLA_HEADS)),
              _pad_lanes(cos_s, 0, LANE), _pad_lanes(sin_s, 0, LANE))
    sel = _key_sel()
    pt_flat = page_table.reshape(-1)
    cache_krt = cache_mla_krope.swapaxes(-1, -2)

    xp = x_prompt.reshape(nb * sl, D_MODEL)
    xs = x_sample.reshape(nseq * ls, D_MODEL)
    outs_p, outs_s = [], []
    for i in range(depth):
        q = _prep_layer({name: w[i] for name, w in params.items()})
        xp, lat, kr, cst, st = _layer_prompt(xp, q, tabs_p, sel, nb, sl)
        outs_p.append((lat.reshape(nb, sl, -1), kr.reshape(nb, sl, -1), cst, st))
        xs, lat, kr, cst, st, vr = _layer_sample(xs, q, tabs_s, nseq, ls, state_conv[i], state_gla,
                                                 cache_mla_latent, cache_krt, pt_flat, i, n_pages)
        outs_s.append((lat.reshape(nseq, ls, -1), kr.reshape(nseq, ls, -1), cst, st, vr.reshape(nseq, ls, -1)))
    stack = lambda outs, j: jnp.stack([o[j] for o in outs])
    return (xp.reshape(nb, sl, D_MODEL), xs.reshape(nseq, ls, D_MODEL),
            stack(outs_p, 0), stack(outs_p, 1), stack(outs_p, 2), stack(outs_p, 3),
            stack(outs_s, 0), stack(outs_s, 1), stack(outs_s, 2), stack(outs_s, 3), stack(outs_s, 4))
```

```python
import functools
import math

import numpy as np
import jax
import jax.numpy as jnp
from jax import lax
from jax.experimental import pallas as pl
from jax.experimental.pallas import tpu as pltpu

F32 = jnp.float32
BF16 = jnp.bfloat16
EPS = 1e-6

D_MODEL = 1024
CONV_DIM = 512
CONV_WIDTH = 31
CONV_HALO = 32
GLA_HEADS = 4
GLA_DK = 64
GLA_DV = 128
GLA_GATE_RANK = 16
GLA_TAU = 16.0
GLA_CHUNK = 64
GLA_DIAG = 4
GLA_SEQS_PROMPT = 8
GLA_SEQS_STEP = 16
GMLP_DIM = 512
GMLP_GROUPS = 4
GMLP_CHUNK = 128
MLA_HEADS = 8
MLA_Q_RANK = 384
MLA_KV_RANK = 256
MLA_NOPE = 64
MLA_ROPE = 32
MLA_V = 64
MLA_SCALE = (MLA_NOPE + MLA_ROPE) ** -0.5
ROPE_THETA = 10000.0
D_FF = 4 * D_MODEL
LANE = 128
SUBLANE = 8
VMEM_LIMIT = 56 * 1024 * 1024

GLA_QK = GLA_HEADS * GLA_DK
GLA_V = GLA_HEADS * GLA_DV
Z_GLA = 2 * GLA_QK + 2 * GLA_V + LANE
Z_MLA = MLA_Q_RANK + MLA_KV_RANK + 2 * LANE


def _params(sem):
    return pltpu.CompilerParams(dimension_semantics=sem, vmem_limit_bytes=VMEM_LIMIT)


def _full(shape):
    n = len(shape)
    return pl.BlockSpec(shape, lambda *_: (0,) * n)


def _rms(x, g):
    return x * lax.rsqrt(jnp.mean(x * x, axis=-1, keepdims=True) + EPS) * g


def _ln(y, g, b):
    mu = jnp.mean(y, axis=-1, keepdims=True)
    yc = y - mu
    var = jnp.mean(yc * yc, axis=-1, keepdims=True)
    return yc * lax.rsqrt(var + EPS) * g + b


def _dot(a, b):
    return jnp.dot(a, b, preferred_element_type=F32)


def _dot_nt(a, b):
    return lax.dot_general(a, b, (((1,), (1,)), ((), ())), preferred_element_type=F32)


def _dot_tn(a, b):
    return lax.dot_general(a, b, (((0,), (0,)), ((), ())), preferred_element_type=F32)


def _inproj_kernel(x_ref, g_ref, w0, w1, w2, w3, o0, o1, o2, o3):
    h = _rms(x_ref[...], g_ref[...]).astype(BF16)
    for w, o in ((w0, o0), (w1, o1), (w2, o2), (w3, o3)):
        o[...] = _dot(h, w[...])


def _inproj(x, g, ws, tm):
    m = x.shape[0]
    return pl.pallas_call(
        _inproj_kernel,
        grid=(m // tm,),
        in_specs=[pl.BlockSpec((tm, D_MODEL), lambda i: (i, 0)), _full((1, D_MODEL))]
        + [_full(w.shape) for w in ws],
        out_specs=[pl.BlockSpec((tm, w.shape[1]), lambda i: (i, 0)) for w in ws],
        out_shape=[jax.ShapeDtypeStruct((m, w.shape[1]), F32) for w in ws],
        compiler_params=_params(("parallel",)),
        name="inproj",
    )(x, g, *ws)


def _conv_post(acc, lg, lb):
    y = _ln(acc, lg, lb)
    return (y * jax.nn.sigmoid(y)).astype(BF16)


def _conv_kernel(z_ref, cw_ref, cb_ref, lg_ref, lb_ref, act_ref, st_ref, ph_sc, *, tl, sub):
    l = pl.program_id(1)

    @pl.when(l == 0)
    def _():
        ph_sc[0, 0:CONV_HALO, :] = jnp.zeros((CONV_HALO, CONV_DIM), F32)

    z = z_ref[...]
    ph_sc[0, CONV_HALO:CONV_HALO + tl, :] = z[:, :CONV_DIM] * jax.nn.sigmoid(z[:, CONV_DIM:])
    first = CONV_HALO - (CONV_WIDTH - 1)
    span = tl + CONV_HALO - SUBLANE
    for r in range(1, SUBLANE):
        for j0 in range(0, span, sub):
            n = min(sub, span - j0)
            ph_sc[r, j0:j0 + n, :] = ph_sc[0, j0 + r:j0 + r + n, :]
    for sb in range(tl // sub):
        acc = jnp.zeros((sub, CONV_DIM), F32) + cb_ref[...]
        for w in range(CONV_WIDTH):
            r, a = (first + w) % SUBLANE, (first + w) // SUBLANE
            acc = acc + ph_sc[r, sb * sub + SUBLANE * a:sb * sub + SUBLANE * a + sub, :] * cw_ref[w:w + 1, :]
        act_ref[sb * sub:(sb + 1) * sub, :] = _conv_post(acc, lg_ref[...], lb_ref[...])

    @pl.when(l == pl.num_programs(1) - 1)
    def _():
        st_ref[...] = ph_sc[0, tl + first:tl + CONV_HALO, :]

    ph_sc[0, 0:CONV_HALO, :] = ph_sc[0, tl:tl + CONV_HALO, :]


def _conv_prompt(z, cw, cb, lg, lb, nb, sl):
    tl = min(512, sl)
    sub = min(64, tl)
    nl = sl // tl
    vec = _full((1, CONV_DIM))
    return pl.pallas_call(
        functools.partial(_conv_kernel, tl=tl, sub=sub),
        grid=(nb, nl),
        in_specs=[pl.BlockSpec((tl, 2 * CONV_DIM), lambda b, l: (b * nl + l, 0)),
                  _full((CONV_WIDTH, CONV_DIM)), vec, vec, vec],
        out_specs=[pl.BlockSpec((tl, CONV_DIM), lambda b, l: (b * nl + l, 0)),
                   pl.BlockSpec((None, CONV_WIDTH - 1, CONV_DIM), lambda b, l: (b, 0, 0))],
        out_shape=[jax.ShapeDtypeStruct((nb * sl, CONV_DIM), BF16),
                   jax.ShapeDtypeStruct((nb, CONV_WIDTH - 1, CONV_DIM), F32)],
        scratch_shapes=[pltpu.VMEM((SUBLANE, tl + CONV_HALO, CONV_DIM), F32)],
        compiler_params=_params(("parallel", "arbitrary")),
        name="conv_prompt",
    )(z, cw, cb, lg, lb)


def _conv_step_kernel(buf_ref, z_ref, cw_ref, cb_ref, lg_ref, lb_ref, act_ref, glu_ref, *, ls):
    nbuf = CONV_WIDTH - 1
    glu = []
    for t in range(ls):
        z = z_ref[t]
        glu.append(z[:, :CONV_DIM] * jax.nn.sigmoid(z[:, CONV_DIM:]))
        glu_ref[t] = glu[t]
    for t in range(ls):
        acc = jnp.zeros(glu[0].shape, F32) + cb_ref[...]
        for w in range(CONV_WIDTH):
            j = t + w
            src = buf_ref[j] if j < nbuf else glu[j - nbuf]
            acc = acc + src * cw_ref[w:w + 1, :]
        act_ref[t] = _conv_post(acc, lg_ref[...], lb_ref[...])


def _conv_step(buf_t, z_t, cw, cb, lg, lb):
    ls, nseq, _ = z_t.shape
    nb = min(32, nseq)
    vec = _full((1, CONV_DIM))
    return pl.pallas_call(
        functools.partial(_conv_step_kernel, ls=ls),
        grid=(nseq // nb,),
        in_specs=[pl.BlockSpec((CONV_WIDTH - 1, nb, CONV_DIM), lambda i: (0, i, 0)),
                  pl.BlockSpec((ls, nb, 2 * CONV_DIM), lambda i: (0, i, 0)),
                  _full((CONV_WIDTH, CONV_DIM)), vec, vec, vec],
        out_specs=[pl.BlockSpec((ls, nb, CONV_DIM), lambda i: (0, i, 0)),
                   pl.BlockSpec((ls, nb, CONV_DIM), lambda i: (0, i, 0))],
        out_shape=[jax.ShapeDtypeStruct((ls, nseq, CONV_DIM), BF16),
                   jax.ShapeDtypeStruct((ls, nseq, CONV_DIM), F32)],
        compiler_params=_params(("parallel",)),
        name="conv_step",
    )(buf_t, z_t, cw, cb, lg, lb)


def _gla_levels(c, diag):
    levels, s = [], c // 2
    while s >= diag:
        levels.append(s)
        s //= 2
    return tuple(levels)


def _gla_sum_masks(c, levels):
    i = np.arange(c)[:, None]
    r = np.arange(c)[None, :]
    mats = [r <= i, r > i]
    for s in levels:
        ref = (i & ~(2 * s - 1)) + s - 1
        upper = (i & s) != 0
        mats.append(np.where(upper, (r > ref) & (r <= i), (r > i) & (r <= ref)))
    return np.concatenate(mats, axis=0).astype(np.float32)


def _gla_kernel(*refs, ns, **kw):
    z_ref, s0_ref, mall_ref, wg_ref, bg_ref, ng_ref, hsum_ref, act_ref, sout_ref, st_sc = refs
    ci = pl.program_id(1)

    @pl.when(ci == 0)
    def _():
        st_sc[...] = s0_ref[...]

    _gla_chunks(z_ref, act_ref, st_sc, mall_ref, wg_ref, bg_ref, ng_ref, hsum_ref, ns=ns, **kw)

    @pl.when(ci == pl.num_programs(1) - 1)
    def _():
        sout_ref[...] = st_sc[...]


def _gla_chunks(z_ref, act_ref, st_sc, mall_ref, wg_ref, bg_ref, ng_ref, hsum_ref, *, ns, c, diag, levels, valid, mmdt):
    seqs = range(ns)
    z = [z_ref[n] for n in seqs]
    q = [zz[:, 0:GLA_QK] * (GLA_DK ** -0.5) for zz in z]
    k = [zz[:, GLA_QK:2 * GLA_QK] for zz in z]
    v = [zz[:, 2 * GLA_QK:2 * GLA_QK + GLA_V] for zz in z]
    wg = wg_ref[...].astype(mmdt)
    x = [_dot(zz[:, 2 * GLA_QK + 2 * GLA_V:].astype(mmdt), wg) + bg_ref[...] for zz in z]
    g = [(jnp.minimum(xx, 0.0) - jnp.log(1.0 + jnp.exp(-jnp.abs(xx)))) * (1.0 / GLA_TAU) for xx in x]
    row = lax.broadcasted_iota(jnp.int32, (c, 1), 0)
    if valid < c:
        g = [jnp.where(row < valid, gg, 0.0) for gg in g]

    parts = []
    for gg in g:
        g1 = gg.astype(BF16)
        r1 = gg - g1.astype(F32)
        g2 = r1.astype(BF16)
        parts += [g1, g2, (r1 - g2.astype(F32)).astype(BF16)]
    e3 = _dot(mall_ref[...].astype(mmdt), jnp.concatenate(parts, axis=1).astype(mmdt))
    w = GLA_QK
    e = [e3[:, 3 * n * w:(3 * n + 1) * w] + e3[:, (3 * n + 1) * w:(3 * n + 2) * w] + e3[:, (3 * n + 2) * w:(3 * n + 3) * w]
         for n in seqs]
    b = [ee[0:c] for ee in e]
    qe = [(q[n] * jnp.exp(b[n])).astype(mmdt) for n in seqs]
    kd = [(k[n] * jnp.exp(e[n][c:2 * c])).astype(mmdt) for n in seqs]
    dlast = [jnp.exp(bb[c - 1:c, :]) for bb in b]
    vm = [vv.astype(mmdt) for vv in v]

    lvl = []
    if levels:
        ri = lax.broadcasted_iota(jnp.int32, (c, c), 0)
        rj = lax.broadcasted_iota(jnp.int32, (c, c), 1)
        for li, s in enumerate(levels):
            pair = ~(2 * s - 1)
            msk = ((ri & s) != 0) & ((rj & s) == 0) & ((ri & pair) == (rj & pair))
            dec = [jnp.exp(e[n][(2 + li) * c:(3 + li) * c]) for n in seqs]
            lvl.append(([(q[n] * dec[n]).astype(mmdt) for n in seqs], [(k[n] * dec[n]).astype(mmdt) for n in seqs], msk))

    dcol = [[jnp.broadcast_to(dlast[n][:, t * LANE:(t + 1) * LANE], (LANE, LANE)).T for t in range(GLA_QK // LANE)]
            for n in seqs]

    outs = [[] for _ in seqs]
    for h in range(GLA_HEADS):
        ks = slice(h * GLA_DK, (h + 1) * GLA_DK)
        vs = slice(h * GLA_DV, (h + 1) * GLA_DV)
        kt, ko = (h * GLA_DK) // LANE, (h * GLA_DK) % LANE
        st = [st_sc[n, h] for n in seqs]
        o_h = [_dot(qe[n][:, ks], st[n].astype(mmdt)) for n in seqs]
        if lvl:
            a = [jnp.zeros((c, c), F32) for _ in seqs]
            for ql, kl, msk in lvl:
                a = [a[n] + jnp.where(msk, _dot_nt(ql[n][:, ks], kl[n][:, ks]), 0.0) for n in seqs]
            o_h = [o_h[n] + _dot(a[n].astype(mmdt), vm[n][:, vs]) for n in seqs]
        for n in seqs:
            st_sc[n, h] = st[n] * dcol[n][kt][ko:ko + GLA_DK, :] + _dot_tn(kd[n][:, ks], vm[n][:, vs])
            outs[n].append(o_h[n])
    o = [jnp.concatenate(outs[n], axis=1) for n in seqs]

    hsum = hsum_ref[...].astype(mmdt)
    for d in range(min(diag, valid)):
        if d == 0:
            t = [q[n] * k[n] for n in seqs]
            vsh = v
        else:
            ok = (row & (diag - 1)) >= d
            dec = [jnp.exp(jnp.where(ok, b[n] - pltpu.roll(b[n], d, 0), 0.0)) for n in seqs]
            t = [jnp.where(ok, q[n] * pltpu.roll(k[n], d, 0) * dec[n], 0.0) for n in seqs]
            vsh = [pltpu.roll(v[n], d, 0) for n in seqs]
        o = [o[n] + _dot(t[n].astype(mmdt), hsum) * vsh[n] for n in seqs]

    ng = ng_ref[...]
    for n in seqs:
        normed = [_rms(o[n][:, h * GLA_DV:(h + 1) * GLA_DV], ng[:, h * GLA_DV:(h + 1) * GLA_DV])
                  for h in range(GLA_HEADS)]
        r = z[n][:, 2 * GLA_QK + GLA_V:2 * GLA_QK + 2 * GLA_V]
        act_ref[n] = (jnp.concatenate(normed, axis=1) * (r * jax.nn.sigmoid(r))).astype(act_ref.dtype)


def _gla(z, s0, layer, wg, bg, ng, nb, sl, c, valid, act_dtype, ns):
    assert nb % ns == 0 and sl % c == 0 and GLA_DV == LANE
    diag = min(GLA_DIAG, c)
    levels = _gla_levels(c, diag)
    mall = jnp.asarray(_gla_sum_masks(c, levels), BF16)
    hs = np.zeros((GLA_QK, GLA_V), np.float32)
    for h in range(GLA_HEADS):
        hs[h * GLA_DK:(h + 1) * GLA_DK, h * GLA_DV:(h + 1) * GLA_DV] = 1.0
    hsum = jnp.asarray(hs, BF16)
    nc = sl // c
    mmdt = BF16 if c % 16 == 0 else F32
    st_shape = (ns, GLA_HEADS, GLA_DK, GLA_DV)
    rows = lambda width: pl.BlockSpec((None, ns, c, width), lambda b, i: (b, 0, i, 0))
    act, st = pl.pallas_call(
        functools.partial(_gla_kernel, ns=ns, c=c, diag=diag, levels=levels, valid=valid, mmdt=mmdt),
        grid=(nb // ns, nc),
        in_specs=[rows(Z_GLA), pl.BlockSpec((None,) + st_shape, lambda b, i: (layer, b, 0, 0, 0)),
                  _full(mall.shape), _full(wg.shape), _full(bg.shape), _full(ng.shape), _full(hsum.shape)],
        out_specs=[rows(GLA_V), pl.BlockSpec(st_shape, lambda b, i: (b, 0, 0, 0))],
        out_shape=[jax.ShapeDtypeStruct((nb // ns, ns, sl, GLA_V), act_dtype),
                   jax.ShapeDtypeStruct((nb, GLA_HEADS, GLA_DK, GLA_DV), F32)],
        scratch_shapes=[pltpu.VMEM(st_shape, F32)],
        compiler_params=_params(("parallel", "arbitrary")),
        name="gla",
    )(z.reshape(nb // ns, ns, sl, Z_GLA), s0, mall, wg, bg, ng, hsum)
    return act.reshape(nb * sl, GLA_V), st


def _gmlp_kernel(z_ref, ws_ref, bsb_ref, lg_ref, lb_ref, act_ref, *v_refs, nchunk):
    ti = lax.broadcasted_iota(jnp.int32, (GMLP_CHUNK, GMLP_CHUNK), 0)
    tj = lax.broadcasted_iota(jnp.int32, (GMLP_CHUNK, GMLP_CHUNK), 1)
    wt = [jnp.where(ti >= tj, ws_ref[g], 0.0).astype(BF16) for g in range(GMLP_GROUPS)]
    gc = GMLP_DIM // GMLP_GROUPS
    for ch in range(nchunk):
        rows = slice(ch * GMLP_CHUNK, (ch + 1) * GMLP_CHUNK)
        z = z_ref[rows, :]
        ge = z * (0.5 * (1.0 + jnp.tanh(math.sqrt(2.0 / math.pi) * (z + 0.044715 * (z * z * z)))))
        v = _ln(ge[:, GMLP_DIM:], lg_ref[...], lb_ref[...])
        if v_refs:
            v_refs[0][rows, :] = v
        vb = v.astype(BF16)
        s = jnp.concatenate([_dot(wt[g], vb[:, g * gc:(g + 1) * gc]) for g in range(GMLP_GROUPS)], axis=1)
        act_ref[rows, :] = (ge[:, :GMLP_DIM] * (s + bsb_ref[...])).astype(BF16)


def _gmlp(z, ws, bsb, lg, lb, emit_v):
    m = z.shape[0]
    tm = min(512, m)
    assert tm % GMLP_CHUNK == 0 and m % tm == 0
    blk = lambda n: pl.BlockSpec((tm, n), lambda i: (i, 0))
    vec = _full((1, GMLP_DIM))
    out_specs = [blk(GMLP_DIM)]
    out_shape = [jax.ShapeDtypeStruct((m, GMLP_DIM), BF16)]
    if emit_v:
        out_specs.append(blk(GMLP_DIM))
        out_shape.append(jax.ShapeDtypeStruct((m, GMLP_DIM), F32))
    return pl.pallas_call(
        functools.partial(_gmlp_kernel, nchunk=tm // GMLP_CHUNK),
        grid=(m // tm,),
        in_specs=[blk(2 * GMLP_DIM), _full(ws.shape), _full(bsb.shape), vec, vec],
        out_specs=out_specs,
        out_shape=out_shape,
        compiler_params=_params(("parallel",)),
        name="gmlp",
    )(z, ws, bsb, lg, lb)


def _mla_latents(z, ck, sk, gkv, lat_ref, kr_ref):
    c = _rms(z[:, MLA_Q_RANK:MLA_Q_RANK + MLA_KV_RANK], gkv)
    lat_ref[...] = c
    o = MLA_Q_RANK + MLA_KV_RANK
    kr = z[:, o:o + LANE] * ck + z[:, o + LANE:o + 2 * LANE] * sk
    kr_ref[...] = kr[:, :MLA_ROPE]
    return c, kr


def _mla_proj_kernel(z_ref, cq_ref, sq_ref, ck_ref, sk_ref, gq_ref, gkv_ref, wq1_ref, wq2_ref, wuk_ref,
                     wuv_ref, sel_ref, q_ref, k_ref, v_ref, lat_ref, kr_ref):
    z = z_ref[...]
    hq = _rms(z[:, :MLA_Q_RANK], gq_ref[...]).astype(BF16)
    q1 = _dot(hq, wq1_ref[...])
    q2 = _dot(hq, wq2_ref[...])
    cosq = cq_ref[...]
    sinq = sq_ref[...]
    for h in range(MLA_HEADS):
        hs = slice(h * LANE, (h + 1) * LANE)
        q_ref[:, hs] = ((q1[:, hs] * cosq + q2[:, hs] * sinq) * MLA_SCALE).astype(BF16)
    c, kr = _mla_latents(z, ck_ref[...], sk_ref[...], gkv_ref[...], lat_ref, kr_ref)
    cb = c.astype(BF16)
    k_ref[...] = (_dot(cb, wuk_ref[...]) + _dot(kr.astype(BF16), sel_ref[...])).astype(BF16)
    v_ref[...] = _dot(cb, wuv_ref[...]).astype(BF16)


def _mla_proj(z, tabs, gq, gkv, wq1, wq2, wuk, wuv, sel, sl):
    m = z.shape[0]
    tm = min(512, sl)
    nl = sl // tm
    row = lambda n: pl.BlockSpec((tm, n), lambda i: (i, 0))
    tab = pl.BlockSpec((tm, LANE), lambda i: (i % nl, 0))
    hd = MLA_HEADS * LANE
    return pl.pallas_call(
        _mla_proj_kernel,
        grid=(m // tm,),
        in_specs=[row(Z_MLA), tab, tab, tab, tab, _full(gq.shape), _full(gkv.shape), _full(wq1.shape),
                  _full(wq2.shape), _full(wuk.shape), _full(wuv.shape), _full(sel.shape)],
        out_specs=[row(hd), row(hd), row(MLA_HEADS * MLA_V), row(MLA_KV_RANK), row(MLA_ROPE)],
        out_shape=[jax.ShapeDtypeStruct((m, hd), BF16), jax.ShapeDtypeStruct((m, hd), BF16),
                   jax.ShapeDtypeStruct((m, MLA_HEADS * MLA_V), BF16),
                   jax.ShapeDtypeStruct((m, MLA_KV_RANK), F32), jax.ShapeDtypeStruct((m, MLA_ROPE), F32)],
        compiler_params=_params(("parallel",)),
        name="mla_proj",
    )(z, *tabs, gq, gkv, wq1, wq2, wuk, wuv, sel)


def _flash_kernel(q_ref, k_ref, v_ref, o_ref, m_sc, acc_sc, *, t, tk):
    qi = pl.program_id(2)
    m_sc[...] = jnp.full(m_sc.shape, -jnp.inf, F32)
    acc_sc[...] = jnp.zeros(acc_sc.shape, F32)
    ones = jnp.ones((tk, LANE), BF16)

    def block(j, r0, nr, triangular):
        rows = slice(r0, r0 + nr)
        start = pl.multiple_of(j * tk, tk)
        kblk = k_ref[pl.ds(start, tk), :]
        vext = jnp.concatenate([v_ref[pl.ds(start, tk), :], ones], axis=1)
        heads = range(2)
        s = [_dot_nt(q_ref[rows, a * LANE:(a + 1) * LANE], kblk[:, a * LANE:(a + 1) * LANE]) for a in heads]
        if triangular:
            ri = lax.broadcasted_iota(jnp.int32, (nr, tk), 0)
            rj = lax.broadcasted_iota(jnp.int32, (nr, tk), 1)
            s = [jnp.where(ri >= rj, ss, -jnp.inf) for ss in s]
        m_prev = [m_sc[a, rows] for a in heads]
        m_new = [jnp.maximum(m_prev[a], jnp.max(s[a], axis=-1, keepdims=True)) for a in heads]
        alpha = [jnp.exp(m_prev[a] - m_new[a]) for a in heads]
        p = [jnp.exp(s[a] - jnp.concatenate([m_new[a]] * (tk // LANE), axis=1)).astype(BF16) for a in heads]
        pv = [_dot(p[a], vext) for a in heads]
        for a in heads:
            acc_sc[a, rows] = jnp.concatenate([alpha[a]] * 2, axis=1) * acc_sc[a, rows] + pv[a]
            m_sc[a, rows] = m_new[a]

    per = t // tk
    nfull = qi * per

    def body(j, carry):
        block(j, 0, t, False)
        return carry

    lax.fori_loop(0, nfull, body, 0)
    for d in range(per):
        block(nfull + d, d * tk, tk, True)
        if (d + 1) * tk < t:
            block(nfull + d, (d + 1) * tk, t - (d + 1) * tk, False)
    lane = lax.broadcasted_iota(jnp.int32, (t, LANE), 1)
    a0, a1 = acc_sc[0], acc_sc[1]
    o_ref[...] = jnp.where(lane < MLA_V, a0[:, :LANE] / a0[:, LANE:], a1[:, :LANE] / a1[:, LANE:]).astype(BF16)


def _flash(q, k, v, nb, sl):
    t = min(1024, sl)
    tk = min(512, t)
    nt = sl // t
    return pl.pallas_call(
        functools.partial(_flash_kernel, t=t, tk=tk),
        grid=(nb, MLA_HEADS // 2, nt),
        in_specs=[pl.BlockSpec((t, 2 * LANE), lambda b, h, i: (b * nt + i, h)),
                  pl.BlockSpec((sl, 2 * LANE), lambda b, h, i: (b, h)),
                  pl.BlockSpec((sl, LANE), lambda b, h, i: (b, h))],
        out_specs=pl.BlockSpec((t, LANE), lambda b, h, i: (b * nt + i, h)),
        out_shape=jax.ShapeDtypeStruct((nb * sl, MLA_HEADS * MLA_V), BF16),
        scratch_shapes=[pltpu.VMEM((2, t, LANE), F32), pltpu.VMEM((2, t, 2 * LANE), F32)],
        compiler_params=_params(("parallel", "parallel", "arbitrary")),
        name="mla_flash",
    )(q, k, v)


def _mla_proj_step_kernel(z_ref, c8_ref, s8_ref, ck_ref, sk_ref, gq_ref, gkv_ref, wqn_ref, wqr_ref, wqs_ref,
                          wukt_ref, qlat_ref, qr_ref, lat_ref, kr_ref):
    z = z_ref[...]
    hq = _rms(z[:, :MLA_Q_RANK], gq_ref[...]).astype(BF16)
    qn = _dot(hq, wqn_ref[...]).astype(BF16)
    for h in range(MLA_HEADS):
        lat = _dot(qn[:, h * LANE:(h + 1) * LANE], wukt_ref[h])
        qlat_ref[:, h * MLA_KV_RANK:(h + 1) * MLA_KV_RANK] = (lat * MLA_SCALE).astype(BF16)
    qr = _dot(hq, wqr_ref[...]) * c8_ref[...] + _dot(hq, wqs_ref[...]) * s8_ref[...]
    qr_ref[...] = (qr * MLA_SCALE).astype(BF16)
    _mla_latents(z, ck_ref[...], sk_ref[...], gkv_ref[...], lat_ref, kr_ref)


def _mla_proj_step(z, tabs, gq, gkv, wqn, wqr, wqs, wukt):
    m = z.shape[0]
    args = (z, *tabs, gq, gkv, wqn, wqr, wqs, wukt)
    return pl.pallas_call(
        _mla_proj_step_kernel,
        grid=(1,),
        in_specs=[_full(a.shape) for a in args],
        out_specs=[_full((m, MLA_HEADS * MLA_KV_RANK)), _full((m, MLA_HEADS * MLA_ROPE)),
                   _full((m, MLA_KV_RANK)), _full((m, MLA_ROPE))],
        out_shape=[jax.ShapeDtypeStruct((m, MLA_HEADS * MLA_KV_RANK), BF16),
                   jax.ShapeDtypeStruct((m, MLA_HEADS * MLA_ROPE), BF16),
                   jax.ShapeDtypeStruct((m, MLA_KV_RANK), F32), jax.ShapeDtypeStruct((m, MLA_ROPE), F32)],
        compiler_params=_params(("arbitrary",)),
        name="mla_proj_step",
    )(*args)


def _paged_kernel(pt_ref, ql_ref, qr_ref, cn_ref, rn_ref, lat_hbm, krt_hbm, o_ref,
                  lat_buf, kr_buf, sem, m_sc, l_sc, acc_sc, kb_sc, *, layer, pp, ls, npad, nsq):
    j = pl.program_id(1)
    nj = pl.num_programs(1)
    step = pl.program_id(0) * nj + j
    nsteps = pl.num_programs(0) * nj
    slot = lax.rem(step, 2)
    seqs = range(nsq)

    def page_copies(at_step, sl):
        group = at_step // nj
        first_page = (at_step - group * nj) * pp
        out = []
        for u in seqs:
            base = (group * nsq + u) * (nj * pp) + first_page
            for i in range(pp):
                pg = pt_ref[base + i]
                out.append(pltpu.make_async_copy(lat_hbm.at[layer, pg], lat_buf.at[sl, u * pp + i], sem.at[0, sl]))
                out.append(pltpu.make_async_copy(krt_hbm.at[layer, pg], kr_buf.at[sl, u * pp + i], sem.at[1, sl]))
        return out

    @pl.when(step == 0)
    def _():
        for cp in page_copies(step, slot):
            cp.start()

    @pl.when(step + 1 < nsteps)
    def _():
        for cp in page_copies(step + 1, 1 - slot):
            cp.start()

    nrow = ql_ref.shape[1]

    @pl.when(j == 0)
    def _():
        for u in seqs:
            cn = cn_ref[u]
            s = _dot_nt(ql_ref[u].astype(F32), cn) + _dot_nt(qr_ref[u].astype(F32), rn_ref[u])
            tok = jnp.right_shift(lax.broadcasted_iota(jnp.int32, (nrow, npad), 0), int(math.log2(MLA_HEADS)))
            key = lax.broadcasted_iota(jnp.int32, (nrow, npad), 1)
            s = jnp.where((key <= tok) & (key < ls), s, -jnp.inf)
            m = jnp.max(s, axis=-1, keepdims=True)
            p = jnp.exp(s - m)
            m_sc[u] = m
            l_sc[u] = jnp.sum(p, axis=-1, keepdims=True)
            acc_sc[u] = _dot(p, cn)

    for cp in page_copies(step, slot):
        cp.wait()
    for u in seqs:
        for i in range(pp):
            kb_sc[u, i * LANE:(i + 1) * LANE, :] = lat_buf[slot, u * pp + i].astype(BF16)
    kb = [kb_sc[u] for u in seqs]
    rb = [jnp.concatenate([kr_buf[slot, u * pp + i] for i in range(pp)], axis=1) for u in seqs]
    s = [_dot_nt(ql_ref[u], kb[u]) + _dot(qr_ref[u].astype(F32), rb[u]) for u in seqs]
    m_prev = [m_sc[u] for u in seqs]
    m_new = [jnp.maximum(m_prev[u], jnp.max(s[u], axis=-1, keepdims=True)) for u in seqs]
    alpha = [jnp.exp(m_prev[u] - m_new[u]) for u in seqs]
    p = [jnp.exp(s[u] - m_new[u]) for u in seqs]
    pv = [_dot(p[u].astype(BF16), kb[u]) for u in seqs]
    for u in seqs:
        l_sc[u] = alpha[u] * l_sc[u] + jnp.sum(p[u], axis=-1, keepdims=True)
        acc_sc[u] = alpha[u] * acc_sc[u] + pv[u]
        m_sc[u] = m_new[u]

    @pl.when(j == nj - 1)
    def _():
        for u in seqs:
            o_ref[u] = (acc_sc[u] / l_sc[u]).astype(BF16)


def _paged_attention(pt_flat, ql, qr, cn, rn, cache_lat, cache_krt, layer, n_pages, ls):
    nseq, nrow, _ = ql.shape
    page = cache_lat.shape[2]
    assert page == LANE
    pp = min(32, n_pages)
    nsq = 2 if nseq % 2 == 0 else 1
    assert n_pages % pp == 0
    npad = cn.shape[1]
    seq = lambda shp: pl.BlockSpec((nsq,) + shp, lambda b, j, pt: (b, 0, 0))
    hbm = pl.BlockSpec(memory_space=pl.ANY)
    grid_spec = pltpu.PrefetchScalarGridSpec(
        num_scalar_prefetch=1,
        grid=(nseq // nsq, n_pages // pp),
        in_specs=[seq((nrow, MLA_KV_RANK)), seq((nrow, MLA_ROPE)), seq((npad, MLA_KV_RANK)),
                  seq((npad, MLA_ROPE)), hbm, hbm],
        out_specs=seq((nrow, MLA_KV_RANK)),
        scratch_shapes=[pltpu.VMEM((2, nsq * pp, page, MLA_KV_RANK), F32),
                        pltpu.VMEM((2, nsq * pp, MLA_ROPE, page), F32),
                        pltpu.SemaphoreType.DMA((2, 2)),
                        pltpu.VMEM((nsq, nrow, 1), F32), pltpu.VMEM((nsq, nrow, 1), F32),
                        pltpu.VMEM((nsq, nrow, MLA_KV_RANK), F32),
                        pltpu.VMEM((nsq, pp * page, MLA_KV_RANK), BF16)],
    )
    return pl.pallas_call(
        functools.partial(_paged_kernel, layer=layer, pp=pp, ls=ls, npad=npad, nsq=nsq),
        grid_spec=grid_spec,
        out_shape=jax.ShapeDtypeStruct((nseq, nrow, MLA_KV_RANK), BF16),
        compiler_params=_params(("arbitrary", "arbitrary")),
        name="mla_paged",
    )(pt_flat, ql, qr, cn, rn, cache_lat, cache_krt)


def _mla_up_kernel(o_ref, w_ref, out_ref):
    acc = _dot(o_ref[:, 0:MLA_KV_RANK], w_ref[0])
    for h in range(1, MLA_HEADS):
        acc = acc + _dot(o_ref[:, h * MLA_KV_RANK:(h + 1) * MLA_KV_RANK], w_ref[h])
    out_ref[...] = acc.astype(BF16)


def _mla_up(o_lat, wuvp):
    m = o_lat.shape[0]
    return pl.pallas_call(
        _mla_up_kernel,
        grid=(1,),
        in_specs=[_full(o_lat.shape), _full(wuvp.shape)],
        out_specs=_full((m, MLA_HEADS * MLA_V)),
        out_shape=jax.ShapeDtypeStruct((m, MLA_HEADS * MLA_V), BF16),
        compiler_params=_params(("arbitrary",)),
        name="mla_up",
    )(o_lat, wuvp)


def _merge_kernel(x_ref, a_ref, b_ref, c_ref, d_ref, g_ref, wg_ref, wa_ref, wb_ref, wc_ref, wd_ref, wo_ref,
                  gp_ref, o_ref):
    x = x_ref[...]
    h = _rms(x, g_ref[...]).astype(BF16)
    merged = None
    for i, (act, w) in enumerate(((a_ref, wa_ref), (b_ref, wb_ref), (c_ref, wc_ref), (d_ref, wd_ref))):
        gate = jax.nn.sigmoid(_dot(h, wg_ref[:, i * D_MODEL:(i + 1) * D_MODEL]))
        term = gate * _dot(act[...], w[...])
        merged = term if merged is None else merged + term
    y = _dot(merged.astype(BF16), wo_ref[...])
    o_ref[...] = x + _rms(y, gp_ref[...])


def _merge(x, acts, g, wgate, wouts, wo, gp):
    m = x.shape[0]
    tm = min(256, m)
    row = lambda n: pl.BlockSpec((tm, n), lambda i: (i, 0))
    vec = _full((1, D_MODEL))
    return pl.pallas_call(
        _merge_kernel,
        grid=(m // tm,),
        in_specs=[row(D_MODEL)] + [row(a.shape[1]) for a in acts] + [vec, _full(wgate.shape)]
        + [_full(w.shape) for w in wouts] + [_full(wo.shape), vec],
        out_specs=row(D_MODEL),
        out_shape=jax.ShapeDtypeStruct((m, D_MODEL), F32),
        compiler_params=_params(("parallel",)),
        name="merge",
    )(x, *acts, g, wgate, *wouts, wo, gp)


def _ffn_kernel(x_ref, g_ref, w1_ref, w2_ref, gp_ref, o_ref, h_sc, acc_sc):
    j = pl.program_id(1)

    @pl.when(j == 0)
    def _():
        h_sc[...] = _rms(x_ref[...], g_ref[...]).astype(BF16)
        acc_sc[...] = jnp.zeros(acc_sc.shape, F32)

    t = jnp.maximum(_dot(h_sc[...], w1_ref[...]), 0.0)
    acc_sc[...] += _dot((t * t).astype(BF16), w2_ref[...])

    @pl.when(j == pl.num_programs(1) - 1)
    def _():
        o_ref[...] = x_ref[...] + _rms(acc_sc[...], gp_ref[...])


def _ffn(x, g, w1, w2, gp):
    m = x.shape[0]
    tm = min(1024, m)
    tf = 1024
    vec = _full((1, D_MODEL))
    return pl.pallas_call(
        _ffn_kernel,
        grid=(m // tm, D_FF // tf),
        in_specs=[pl.BlockSpec((tm, D_MODEL), lambda i, j: (i, 0)), vec,
                  pl.BlockSpec((D_MODEL, tf), lambda i, j: (0, j)),
                  pl.BlockSpec((tf, D_MODEL), lambda i, j: (j, 0)), vec],
        out_specs=pl.BlockSpec((tm, D_MODEL), lambda i, j: (i, 0)),
        out_shape=jax.ShapeDtypeStruct((m, D_MODEL), F32),
        scratch_shapes=[pltpu.VMEM((tm, D_MODEL), BF16), pltpu.VMEM((tm, D_MODEL), F32)],
        compiler_params=_params(("parallel", "arbitrary")),
        name="ffn",
    )(x, g, w1, w2, gp)


def _swap_halves(w):
    half = w.shape[-1] // 2
    return jnp.concatenate([w[..., half:], w[..., :half]], axis=-1)


def _prep_layer(p):
    w_in = p['w_in']
    o_gla = 2 * CONV_DIM
    o_gd = o_gla + 2 * GLA_QK + GLA_V
    o_r = o_gd + GLA_GATE_RANK
    o_gmlp = o_r + GLA_V
    o_mla = o_gmlp + 2 * GMLP_DIM
    o_kr = o_mla + MLA_Q_RANK + MLA_KV_RANK
    o_gate = o_kr + MLA_ROPE
    rows = w_in.shape[0]
    zpad = lambda n: jnp.zeros((rows, n), w_in.dtype)
    w_kr = w_in[:, o_kr:o_gate]
    q = {}
    q['w_conv'] = w_in[:, :o_gla].astype(BF16)
    q['w_gla'] = jnp.concatenate([w_in[:, o_gla:o_gd], w_in[:, o_r:o_gmlp], w_in[:, o_gd:o_r],
                                  zpad(LANE - GLA_GATE_RANK)], axis=1).astype(BF16)
    q['w_gmlp'] = w_in[:, o_gmlp:o_mla].astype(BF16)
    q['w_mla'] = jnp.concatenate([w_in[:, o_mla:o_kr], w_kr, zpad(LANE - MLA_ROPE), _swap_halves(w_kr),
                                  zpad(LANE - MLA_ROPE)], axis=1).astype(BF16)
    q['w_gate'] = w_in[:, o_gate:].astype(BF16)

    q['gla_wg'] = jnp.concatenate([p['gla_gate_up'], jnp.zeros((LANE - GLA_GATE_RANK, GLA_QK), F32)],
                                  axis=0).astype(BF16)
    q['gla_bg'] = p['gla_gate_b'].reshape(1, GLA_QK)
    q['gla_ng'] = p['gla_norm_g'].reshape(1, GLA_V)

    rq = MLA_Q_RANK
    wq = p['mla_w_uq'].reshape(rq, MLA_HEADS, MLA_NOPE + MLA_ROPE)
    nope, rope = wq[..., :MLA_NOPE], wq[..., MLA_NOPE:]
    rope_sw = _swap_halves(rope)
    zq = lambda n: jnp.zeros((rq, MLA_HEADS, n), F32)
    hd = MLA_HEADS * LANE
    q['wq1'] = jnp.concatenate([nope, rope, zq(LANE - MLA_NOPE - MLA_ROPE)], -1).reshape(rq, hd).astype(BF16)
    q['wq2'] = jnp.concatenate([zq(MLA_NOPE), rope_sw, zq(LANE - MLA_NOPE - MLA_ROPE)], -1).reshape(rq, hd).astype(BF16)
    q['wqn'] = jnp.concatenate([nope, zq(LANE - MLA_NOPE)], -1).reshape(rq, hd).astype(BF16)
    q['wqr'] = rope.reshape(rq, MLA_HEADS * MLA_ROPE).astype(BF16)
    q['wqs'] = rope_sw.reshape(rq, MLA_HEADS * MLA_ROPE).astype(BF16)
    wuk, wuv = p['mla_w_uk'], p['mla_w_uv']
    zk = jnp.zeros((MLA_KV_RANK, MLA_HEADS, LANE - MLA_NOPE), F32)
    q['wuk'] = jnp.concatenate([wuk, zk], -1).reshape(MLA_KV_RANK, hd).astype(BF16)
    q['wuv'] = wuv.reshape(MLA_KV_RANK, MLA_HEADS * MLA_V).astype(BF16)
    q['wukt'] = jnp.concatenate([wuk, zk], -1).transpose(1, 2, 0).astype(BF16)
    eye = jnp.eye(MLA_HEADS, dtype=F32)
    q['wuvp'] = (wuv.transpose(1, 0, 2)[:, :, None, :] * eye[:, None, :, None]).reshape(
        MLA_HEADS, MLA_KV_RANK, MLA_HEADS * MLA_V).astype(BF16)

    for name in ('conv_out', 'gla_out', 'gmlp_out', 'mla_out', 'w_o', 'ffn_w1', 'ffn_w2'):
        q[name] = p[name].astype(BF16)
    for name in ('g_pre_mix', 'g_post_mix', 'g_pre_ffn', 'g_post_ffn', 'conv_b', 'conv_ln_g', 'conv_ln_b',
                 'gmlp_ln_g', 'gmlp_ln_b', 'mla_q_norm_g', 'mla_kv_norm_g'):
        q[name] = p[name].reshape(1, -1)
    q['conv_w'] = p['conv_w']
    q['gmlp_ws'] = p['gmlp_ws']
    q['gmlp_bs'] = p['gmlp_bs']
    return q


def _rope_tables(pos):
    half = MLA_ROPE // 2
    inv = ROPE_THETA ** (-jnp.arange(half, dtype=F32) / half)
    ang = pos.astype(F32)[:, None] * inv[None, :]
    cos, sin = jnp.cos(ang), jnp.sin(ang)
    return jnp.concatenate([cos, cos], -1), jnp.concatenate([-sin, sin], -1)


def _key_sel():
    sel = np.zeros((LANE, MLA_HEADS * LANE), np.float32)
    for h in range(MLA_HEADS):
        for j in range(MLA_ROPE):
            sel[j, h * LANE + MLA_NOPE + j] = 1.0
    return jnp.asarray(sel, BF16)


def _pad_lanes(t, left, total):
    n = t.shape[0]
    return jnp.concatenate([jnp.zeros((n, left), F32), t, jnp.zeros((n, total - left - t.shape[1]), F32)], -1)


def _mix_tail(x, acts, q):
    x = _merge(x, acts, q['g_pre_mix'], q['w_gate'],
               (q['conv_out'], q['gla_out'], q['gmlp_out'], q['mla_out']), q['w_o'], q['g_post_mix'])
    return _ffn(x, q['g_pre_ffn'], q['ffn_w1'], q['ffn_w2'], q['g_post_ffn'])


def _layer_prompt(x, q, tabs, sel, nb, sl):
    m = nb * sl
    z_conv, z_gla, z_gmlp, z_mla = _inproj(x, q['g_pre_mix'], (q['w_conv'], q['w_gla'], q['w_gmlp'], q['w_mla']),
                                           min(256, m))
    act_a, conv_st = _conv_prompt(z_conv, q['conv_w'], q['conv_b'], q['conv_ln_g'], q['conv_ln_b'], nb, sl)
    c = min(GLA_CHUNK, sl)
    s0 = jnp.zeros((1, nb, GLA_HEADS, GLA_DK, GLA_DV), F32)
    act_b, st = _gla(z_gla, s0, 0, q['gla_wg'], q['gla_bg'], q['gla_ng'], nb, sl, c, c, BF16,
                     GLA_SEQS_PROMPT if nb % GLA_SEQS_PROMPT == 0 else 1)
    bsb = jnp.repeat(q['gmlp_bs'].T, GMLP_DIM // GMLP_GROUPS, axis=1)
    (act_c,) = _gmlp(z_gmlp, q['gmlp_ws'], bsb, q['gmlp_ln_g'], q['gmlp_ln_b'], False)
    qp, kp, v, lat, kr = _mla_proj(z_mla, tabs, q['mla_q_norm_g'], q['mla_kv_norm_g'], q['wq1'], q['wq2'],
                                   q['wuk'], q['wuv'], sel, sl)
    act_d = _flash(qp, kp, v, nb, sl)
    x = _mix_tail(x, (act_a, act_b, act_c, act_d), q)
    return x, lat, kr, conv_st, st


def _layer_sample(x, q, tabs, nseq, ls, conv_buf, gla_states, cache_lat, cache_kr, pt_flat, layer, n_pages):
    m = nseq * ls
    z_conv, z_gla, z_gmlp, z_mla = _inproj(x, q['g_pre_mix'], (q['w_conv'], q['w_gla'], q['w_gmlp'], q['w_mla']),
                                           min(256, m))
    act_t, glu_t = _conv_step(conv_buf.transpose(1, 0, 2), z_conv.reshape(nseq, ls, -1).transpose(1, 0, 2),
                              q['conv_w'], q['conv_b'], q['conv_ln_g'], q['conv_ln_b'])
    act_a = act_t.transpose(1, 0, 2).reshape(m, CONV_DIM)
    conv_st = jnp.concatenate([conv_buf, glu_t.transpose(1, 0, 2)], axis=1)[:, -(CONV_WIDTH - 1):]

    lp = 8
    pad = lambda z: jnp.pad(z.reshape(nseq, ls, -1), ((0, 0), (0, lp - ls), (0, 0))).reshape(nseq * lp, -1)
    unpad = lambda a: a.reshape(nseq, lp, -1)[:, :ls].reshape(m, -1)
    act_b, st = _gla(pad(z_gla), gla_states, layer, q['gla_wg'], q['gla_bg'], q['gla_ng'], nseq, lp, lp, ls, F32,
                     GLA_SEQS_STEP if nseq % GLA_SEQS_STEP == 0 else 1)
    act_b = unpad(act_b).astype(BF16)

    per = GMLP_CHUNK // lp
    eye = jnp.eye(per, dtype=F32)
    wk = (eye[None, :, None, :, None] * q['gmlp_ws'][:, None, :lp, None, :lp]).reshape(
        GMLP_GROUPS, GMLP_CHUNK, GMLP_CHUNK)
    bsb = jnp.repeat(jnp.tile(q['gmlp_bs'][:, :lp], (1, per)).T, GMLP_DIM // GMLP_GROUPS, axis=1)
    act_c, v_rows = _gmlp(pad(z_gmlp), wk, bsb, q['gmlp_ln_g'], q['gmlp_ln_b'], True)
    act_c, v_rows = unpad(act_c), unpad(v_rows)

    ql, qr, lat, kr = _mla_proj_step(z_mla, tabs, q['mla_q_norm_g'], q['mla_kv_norm_g'], q['wqn'], q['wqr'],
                                     q['wqs'], q['wukt'])
    npad = 8
    padk = lambda t: jnp.pad(t.reshape(nseq, ls, -1), ((0, 0), (0, npad - ls), (0, 0)))
    o_lat = _paged_attention(pt_flat, ql.reshape(nseq, ls * MLA_HEADS, MLA_KV_RANK),
                             qr.reshape(nseq, ls * MLA_HEADS, MLA_ROPE), padk(lat), padk(kr),
                             cache_lat, cache_kr, layer, n_pages, ls)
    act_d = _mla_up(o_lat.reshape(m, MLA_HEADS * MLA_KV_RANK), q['wuvp'])
    x = _mix_tail(x, (act_a, act_b, act_c, act_d), q)
    return x, lat, kr, conv_st, st, v_rows


def kernel(x_prompt, x_sample, cache_mla_latent, cache_mla_krope, state_conv, state_gla, page_table,
           g_pre_mix, w_in, conv_w, conv_b, conv_ln_g, conv_ln_b, conv_out,
           gla_gate_up, gla_gate_b, gla_norm_g, gla_out,
           gmlp_ln_g, gmlp_ln_b, gmlp_ws, gmlp_bs, gmlp_out,
           mla_q_norm_g, mla_w_uq, mla_kv_norm_g, mla_w_uk, mla_w_uv, mla_out,
           w_o, g_post_mix, g_pre_ffn, ffn_w1, ffn_w2, g_post_ffn):
    params = {'g_pre_mix': g_pre_mix, 'w_in': w_in, 'conv_w': conv_w, 'conv_b': conv_b, 'conv_ln_g': conv_ln_g,
              'conv_ln_b': conv_ln_b, 'conv_out': conv_out, 'gla_gate_up': gla_gate_up, 'gla_gate_b': gla_gate_b,
              'gla_norm_g': gla_norm_g, 'gla_out': gla_out, 'gmlp_ln_g': gmlp_ln_g, 'gmlp_ln_b': gmlp_ln_b,
              'gmlp_ws': gmlp_ws, 'gmlp_bs': gmlp_bs, 'gmlp_out': gmlp_out, 'mla_q_norm_g': mla_q_norm_g,
              'mla_w_uq': mla_w_uq, 'mla_kv_norm_g': mla_kv_norm_g, 'mla_w_uk': mla_w_uk, 'mla_w_uv': mla_w_uv,
              'mla_out': mla_out, 'w_o': w_o, 'g_post_mix': g_post_mix, 'g_pre_ffn': g_pre_ffn,
              'ffn_w1': ffn_w1, 'ffn_w2': ffn_w2, 'g_post_ffn': g_post_ffn}
    depth = w_in.shape[0]
    nb, sl, _ = x_prompt.shape
    nseq, ls, _ = x_sample.shape
    n_pages = page_table.shape[1]
    past_len = n_pages * cache_mla_latent.shape[2]

    cos_p, sin_p = _rope_tables(jnp.arange(sl, dtype=jnp.int32))
    tabs_p = (jnp.concatenate([jnp.ones((sl, MLA_NOPE), F32), cos_p, jnp.zeros((sl, LANE - MLA_NOPE - MLA_ROPE), F32)], -1),
              _pad_lanes(sin_p, MLA_NOPE, LANE), _pad_lanes(cos_p, 0, LANE), _pad_lanes(sin_p, 0, LANE))
    cos_s, sin_s = _rope_tables(past_len + jnp.arange(ls, dtype=jnp.int32))
    cos_s, sin_s = jnp.tile(cos_s, (nseq, 1)), jnp.tile(sin_s, (nseq, 1))
    tabs_s = (jnp.tile(cos_s, (1, MLA_HEADS)), jnp.tile(sin_s, (1, MLA_HEADS)),
              _pad_lanes(cos_s, 0, LANE), _pad_lanes(sin_s, 0, LANE))
    sel = _key_sel()
    pt_flat = page_table.reshape(-1)
    cache_krt = cache_mla_krope.swapaxes(-1, -2)

    xp = x_prompt.reshape(nb * sl, D_MODEL)
    xs = x_sample.reshape(nseq * ls, D_MODEL)
    outs_p, outs_s = [], []
    for i in range(depth):
        q = _prep_layer({name: w[i] for name, w in params.items()})
        xp, lat, kr, cst, st = _layer_prompt(xp, q, tabs_p, sel, nb, sl)
        outs_p.append((lat.reshape(nb, sl, -1), kr.reshape(nb, sl, -1), cst, st))
        xs, lat, kr, cst, st, vr = _layer_sample(xs, q, tabs_s, nseq, ls, state_conv[i], state_gla,
                                                 cache_mla_latent, cache_krt, pt_flat, i, n_pages)
        outs_s.append((lat.reshape(nseq, ls, -1), kr.reshape(nseq, ls, -1), cst, st, vr.reshape(nseq, ls, -1)))
    stack = lambda outs, j: jnp.stack([o[j] for o in outs])
    return (xp.reshape(nb, sl, D_MODEL), xs.reshape(nseq, ls, D_MODEL),
            stack(outs_p, 0), stack(outs_p, 1), stack(outs_p, 2), stack(outs_p, 3),
            stack(outs_s, 0), stack(outs_s, 1), stack(outs_s, 2), stack(outs_s, 3), stack(outs_s, 4))
```
